```python
import jax, jax.numpy as jnp
from jax import lax
import numpy as np

D_MODEL = 1024
BATCH = 8
SEQ = 4096
DEPTH = 4
DEC_BATCH = 8
DEC_SEQ = 16
PAST_LEN = 2048

CHUNK = 64
N_MIXERS = 2
N_GMLP = (DEPTH + 1) // 2
N_FOX = DEPTH // 2
EXPAND = 2
D_BRANCH = EXPAND * D_MODEL
GMLP_BLOCK = 128
GMLP_GROUPS = 16
GMLP_GROUP_DIM = D_BRANCH // GMLP_GROUPS
FOX_HEADS = 16
FOX_HEAD_DIM = D_BRANCH // FOX_HEADS
Q_BLOCK = 128
PLE_DIM = 256
FORGET_BIAS_INIT = 4.0
RMS_EPS = 1e-6
LN_EPS = 1e-5

kernel_name = "hybrid_gmlp_fox_streaming_step"


def rms_norm(x, g):
    xf = x.astype(jnp.float32)
    y = xf * lax.rsqrt(jnp.mean(xf * xf, axis=-1, keepdims=True) + RMS_EPS)
    return (y * g.astype(jnp.float32)).astype(x.dtype)


def layer_norm(x, g, b):
    xf = x.astype(jnp.float32)
    mu = jnp.mean(xf, axis=-1, keepdims=True)
    var = jnp.mean(jnp.square(xf - mu), axis=-1, keepdims=True)
    y = (xf - mu) * lax.rsqrt(var + LN_EPS) * g.astype(jnp.float32) + b.astype(jnp.float32)
    return y.astype(x.dtype)


def chunk_causal_mask(n):
    i = jnp.arange(n)
    return (i[None, :] // CHUNK) <= (i[:, None] // CHUNK)


def gmlp_branch(h, w_in, ln_g, ln_b, w_s, b_s, w_out):
    B, T, _ = h.shape
    E = D_BRANCH
    proj = h @ w_in
    uv = jax.nn.gelu(proj[..., :2 * E])
    z = proj[..., 2 * E:]
    u = uv[..., :E]
    v = layer_norm(uv[..., E:], ln_g, ln_b)
    n = min(T, GMLP_BLOCK)
    nb = T // n
    wm = jnp.where(chunk_causal_mask(n)[None], w_s[:, :n, :n], 0.0).astype(v.dtype)
    vb = v.reshape(B, nb, n, GMLP_GROUPS, GMLP_GROUP_DIM)
    s = jnp.einsum('gij,bnjgc->bnigc', wm, vb) + b_s[:, :n].T[None, None, :, :, None].astype(v.dtype)
    y = u * s.reshape(B, T, E) * jax.nn.silu(z)
    return y @ w_out, v


def fox_project(h, w_in, b_f):
    B, T, _ = h.shape
    E = D_BRANCH
    proj = h @ w_in
    shp = (B, T, FOX_HEADS, FOX_HEAD_DIM)
    q = proj[..., :E].reshape(shp)
    k = proj[..., E:2 * E].reshape(shp)
    v = proj[..., 2 * E:3 * E].reshape(shp)
    z = proj[..., 3 * E:4 * E]
    logf = jax.nn.log_sigmoid(proj[..., 4 * E:].astype(jnp.float32) + b_f.astype(jnp.float32))
    return q, k, v, z, logf


def fox_attend_prompt(q, k, v, logf):
    B, S, H, dh = q.shape
    scale = dh ** -0.5
    cT = jnp.cumsum(logf, axis=1).transpose(0, 2, 1)
    nq = S // Q_BLOCK
    qb = q.reshape(B, nq, Q_BLOCK, H, dh).transpose(1, 0, 2, 3, 4)
    cqb = cT.reshape(B, H, nq, Q_BLOCK).transpose(2, 0, 1, 3)
    key_pos = jnp.arange(S)

    def block(args):
        qi, ci, start = args
        s = jnp.einsum('bqhd,bkhd->bhqk', qi, k).astype(jnp.float32) * scale
        s = s + ci[..., :, None] - cT[:, :, None, :]
        qpos = start + jnp.arange(Q_BLOCK)
        s = jnp.where(key_pos[None, :] <= qpos[:, None], s, -jnp.inf)
        p = jax.nn.softmax(s, axis=-1).astype(v.dtype)
        return jnp.einsum('bhqk,bkhd->bqhd', p, v)

    o = lax.map(block, (qb, cqb, jnp.arange(nq, dtype=jnp.int32) * Q_BLOCK))
    return o.transpose(1, 0, 2, 3, 4).reshape(B, S, H * dh)


def fox_attend_sample(q, k_new, v_new, logf_new, k_cache, v_cache, logf_cache):
    B, T, H, dh = q.shape
    P = k_cache.shape[1]
    scale = dh ** -0.5
    k = jnp.concatenate([k_cache.astype(k_new.dtype), k_new], axis=1)
    v = jnp.concatenate([v_cache.astype(v_new.dtype), v_new], axis=1)
    cT = jnp.cumsum(jnp.concatenate([logf_cache.astype(jnp.float32), logf_new], axis=1), axis=1).transpose(0, 2, 1)
    s = jnp.einsum('bqhd,bkhd->bhqk', q, k).astype(jnp.float32) * scale
    s = s + cT[:, :, P:, None] - cT[:, :, None, :]
    mask = jnp.arange(P + T)[None, :] <= (P + jnp.arange(T))[:, None]
    s = jnp.where(mask, s, -jnp.inf)
    p = jax.nn.softmax(s, axis=-1).astype(v.dtype)
    return jnp.einsum('bhqk,bkhd->bqhd', p, v).reshape(B, T, H * dh)


def setup_inputs(seed: int = 0) -> dict:
    key = jax.random.key(seed)
    ks = jax.random.split(key, 20)
    f32 = jnp.float32
    E = D_BRANCH

    def nrm(k, shape, scale=1.0):
        return jax.random.normal(k, shape, f32) * scale

    return {
        "x_prompt": nrm(ks[0], (BATCH, SEQ, D_MODEL)),
        "x_sample": nrm(ks[1], (DEC_BATCH, DEC_SEQ, D_MODEL)),
        "cache_fox_k": nrm(ks[2], (N_FOX, DEC_BATCH, PAST_LEN, FOX_HEADS, FOX_HEAD_DIM)),
        "cache_fox_v": nrm(ks[3], (N_FOX, DEC_BATCH, PAST_LEN, FOX_HEADS, FOX_HEAD_DIM)),
        "cache_fox_logf": jax.nn.log_sigmoid(FORGET_BIAS_INIT + nrm(ks[4], (N_FOX, DEC_BATCH, PAST_LEN, FOX_HEADS))),
        "p_prompt": nrm(ks[5], (DEPTH, BATCH, SEQ, PLE_DIM)),
        "p_sample": nrm(ks[6], (DEPTH, DEC_BATCH, DEC_SEQ, PLE_DIM)),
        "norm_pre": 1.0 + 0.1 * nrm(ks[7], (DEPTH, D_MODEL)),
        "norm_post": 1.0 + 0.1 * nrm(ks[8], (DEPTH, D_MODEL)),
        "gmlp_w_in": nrm(ks[9], (N_GMLP, D_MODEL, 3 * E), D_MODEL ** -0.5),
        "gmlp_ln_g": 1.0 + 0.1 * nrm(ks[10], (N_GMLP, E)),
        "gmlp_ln_b": 0.02 * nrm(ks[11], (N_GMLP, E)),
        "gmlp_w_s": nrm(ks[12], (N_GMLP, GMLP_GROUPS, GMLP_BLOCK, GMLP_BLOCK), GMLP_BLOCK ** -0.5),
        "gmlp_b_s": 1.0 + 0.1 * nrm(ks[13], (N_GMLP, GMLP_GROUPS, GMLP_BLOCK)),
        "gmlp_w_out": nrm(ks[14], (N_GMLP, E, D_MODEL), E ** -0.5),
        "fox_w_in": nrm(ks[15], (N_FOX, D_MODEL, 4 * E + FOX_HEADS), D_MODEL ** -0.5),
        "fox_b_f": FORGET_BIAS_INIT + 0.5 * nrm(ks[16], (N_FOX, FOX_HEADS)),
        "fox_w_out": nrm(ks[17], (N_FOX, E, D_MODEL), E ** -0.5),
        "ple_w_proj": nrm(ks[18], (DEPTH, PLE_DIM, D_MODEL), PLE_DIM ** -0.5),
        "ple_w_gate": nrm(ks[19], (DEPTH, D_MODEL, D_MODEL), D_MODEL ** -0.5),
    }


def reference(x_prompt, x_sample, cache_fox_k, cache_fox_v, cache_fox_logf, p_prompt, p_sample,
              norm_pre, norm_post, gmlp_w_in, gmlp_ln_g, gmlp_ln_b, gmlp_w_s, gmlp_b_s, gmlp_w_out,
              fox_w_in, fox_b_f, fox_w_out, ple_w_proj, ple_w_gate):
    xp, xs = x_prompt, x_sample
    gmlp_v_s = []
    fk_p, fv_p, flf_p = [], [], []
    fk_s, fv_s, flf_s = [], [], []
    for i in range(DEPTH):
        j = i // N_MIXERS
        hp = rms_norm(xp, norm_pre[i])
        hs = rms_norm(xs, norm_pre[i])
        if i % N_MIXERS == 0:
            op, _ = gmlp_branch(hp, gmlp_w_in[j], gmlp_ln_g[j], gmlp_ln_b[j], gmlp_w_s[j], gmlp_b_s[j], gmlp_w_out[j])
            os_, vs = gmlp_branch(hs, gmlp_w_in[j], gmlp_ln_g[j], gmlp_ln_b[j], gmlp_w_s[j], gmlp_b_s[j], gmlp_w_out[j])
            gmlp_v_s.append(vs)
        else:
            qp, kp, vp, zp, lfp = fox_project(hp, fox_w_in[j], fox_b_f[j])
            op = (fox_attend_prompt(qp, kp, vp, lfp) * jax.nn.silu(zp)) @ fox_w_out[j]
            qs, ks_, vs_, zs, lfs = fox_project(hs, fox_w_in[j], fox_b_f[j])
            att_s = fox_attend_sample(qs, ks_, vs_, lfs, cache_fox_k[j], cache_fox_v[j], cache_fox_logf[j])
            os_ = (att_s * jax.nn.silu(zs)) @ fox_w_out[j]
            fk_p.append(kp); fv_p.append(vp); flf_p.append(lfp)
            fk_s.append(ks_); fv_s.append(vs_); flf_s.append(lfs)
        xp = xp + rms_norm(op, norm_post[i])
        xs = xs + rms_norm(os_, norm_post[i])
        xp = xp + jax.nn.sigmoid(xp @ ple_w_gate[i]) * (p_prompt[i] @ ple_w_proj[i])
        xs = xs + jax.nn.sigmoid(xs @ ple_w_gate[i]) * (p_sample[i] @ ple_w_proj[i])
    state_gmlp_v_sample = jnp.stack(gmlp_v_s)
    fox_k_prompt = jnp.stack(fk_p)
    fox_v_prompt = jnp.stack(fv_p)
    fox_logf_prompt = jnp.stack(flf_p)
    fox_k_sample = jnp.stack(fk_s)
    fox_v_sample = jnp.stack(fv_s)
    fox_logf_sample = jnp.stack(flf_s)
    return (xp, xs, state_gmlp_v_sample, fox_k_prompt, fox_v_prompt, fox_logf_prompt, fox_k_sample, fox_v_sample, fox_logf_sample)
```

```python
import functools

import jax
import jax.numpy as jnp
from jax import lax
from jax.experimental import pallas as pl
from jax.experimental.pallas import tpu as pltpu

F32 = jnp.float32
BF16 = jnp.bfloat16

CHUNK = 64
GMLP_BLOCK = 128
RMS_EPS = 1e-6
LN_EPS = 1e-5
LANES = 128
VMEM_LIMIT_BYTES = 56 * 1024 * 1024

GMLP_TOKENS = 256
PROJ_TOKENS = 256
POST_TOKENS = 512
ATTN_TILE = 256
CACHE_HEADS = 4
CUMSUM_CHUNK = 256

_NT = (((1,), (1,)), ((), ()))


def _params(n_axes):
    return pltpu.CompilerParams(
        dimension_semantics=("arbitrary",) * n_axes, vmem_limit_bytes=VMEM_LIMIT_BYTES)


def _resident(shape):
    zeros = (0,) * len(shape)
    return pl.BlockSpec(shape, lambda *_: zeros, pipeline_mode=pl.Buffered(1))


def _dot(a, b):
    return jnp.dot(a, b, preferred_element_type=F32)


def _rms_norm(x, g):
    return x * lax.rsqrt(jnp.mean(x * x, axis=-1, keepdims=True) + RMS_EPS) * g


def _silu(z):
    return z * jax.nn.sigmoid(z)


def _split3(x):
    a = x.astype(BF16)
    r = x - a.astype(F32)
    b = r.astype(BF16)
    c = (r - b.astype(F32)).astype(BF16)
    return a, b, c


def _dot_ones_lhs(ones, x):
    a, b, c = _split3(x)
    return (_dot(ones, a) + _dot(ones, b)) + _dot(ones, c)


def _dot_ones_rhs(x, ones):
    a, b, c = _split3(x)
    return (_dot(a, ones) + _dot(b, ones)) + _dot(c, ones)


def _tri(n, upper):
    r = lax.broadcasted_iota(jnp.int32, (n, n), 0)
    c = lax.broadcasted_iota(jnp.int32, (n, n), 1)
    keep = (r <= c) if upper else (c <= r)
    return jnp.where(keep, 1.0, 0.0).astype(BF16)


def _residual_and_embedding(x, o, p, gpost_ref, wgate_ref, wproj_ref):
    x1 = x + _rms_norm(o, gpost_ref[...])
    gate = jax.nn.sigmoid(_dot(x1.astype(BF16), wgate_ref[...]))
    return x1 + gate * _dot(p.astype(BF16), wproj_ref[...])


def _gmlp_layer_kernel(x_ref, p_ref, gpre_ref, gpost_ref, win_ref, lng_ref, lnb_ref,
                       wmix_ref, bmix_ref, wout_ref, wgate_ref, wproj_ref, *rest,
                       span_shift, causal, emit_v):
    if emit_v:
        out_ref, v_ref, y_ref = rest
    else:
        out_ref, y_ref = rest
    tm = x_ref.shape[0]
    e = wout_ref.shape[0]
    groups, blk, _ = wmix_ref.shape
    gd = e // groups

    x = x_ref[...]
    h = _rms_norm(x, gpre_ref[...]).astype(BF16)
    u = jax.nn.gelu(_dot(h, win_ref[:, 0:e]))
    gv = jax.nn.gelu(_dot(h, win_ref[:, e:2 * e]))
    sz = _silu(_dot(h, win_ref[:, 2 * e:3 * e]))
    d = gv - jnp.mean(gv, axis=-1, keepdims=True)
    v = d * lax.rsqrt(jnp.mean(d * d, axis=-1, keepdims=True) + LN_EPS) * lng_ref[...] + lnb_ref[...]
    if emit_v:
        v_ref[...] = v
    vb = v.astype(BF16)

    qs = lax.shift_right_logical(lax.broadcasted_iota(jnp.int32, (blk, blk), 0), span_shift)
    ks = lax.shift_right_logical(lax.broadcasted_iota(jnp.int32, (blk, blk), 1), span_shift)
    allowed = (ks <= qs) if causal else (ks == qs)
    for g in range(groups):
        cs = slice(g * gd, (g + 1) * gd)
        wm = jnp.where(allowed, wmix_ref[g], 0.0).astype(BF16)
        for r in range(tm // blk):
            rs = slice(r * blk, (r + 1) * blk)
            s = _dot(wm, vb[rs, cs]) + bmix_ref[:, cs]
            y_ref[rs, cs] = (u[rs, cs] * s * sz[rs, cs]).astype(BF16)
    o = _dot(y_ref[...], wout_ref[...])
    out_ref[...] = _residual_and_embedding(x, o, p_ref[...], gpost_ref, wgate_ref, wproj_ref)


def _gmlp_layer(x, p, gpre, gpost, w_in, ln_g, ln_b, w_mix, b_mix, w_out, w_gate, w_proj,
                *, tokens, span, causal, emit_v):
    n, d = x.shape
    e = w_out.shape[0]
    pd = p.shape[1]
    assert n % tokens == 0 and tokens % GMLP_BLOCK == 0 and span & (span - 1) == 0
    row = lambda i: (i, 0)
    out_shape = [jax.ShapeDtypeStruct((n, d), F32)]
    out_specs = [pl.BlockSpec((tokens, d), row)]
    if emit_v:
        out_shape.append(jax.ShapeDtypeStruct((n, e), F32))
        out_specs.append(pl.BlockSpec((tokens, e), row))
    res = pl.pallas_call(
        functools.partial(_gmlp_layer_kernel, span_shift=span.bit_length() - 1,
                          causal=causal, emit_v=emit_v),
        grid=(n // tokens,),
        in_specs=[
            pl.BlockSpec((tokens, d), row),
            pl.BlockSpec((tokens, pd), row),
            _resident(gpre.shape), _resident(gpost.shape), _resident(w_in.shape),
            _resident(ln_g.shape), _resident(ln_b.shape), _resident(w_mix.shape),
            _resident(b_mix.shape), _resident(w_out.shape), _resident(w_gate.shape),
            _resident(w_proj.shape),
        ],
        out_specs=out_specs,
        out_shape=out_shape,
        scratch_shapes=[pltpu.VMEM((tokens, e), BF16)],
        compiler_params=_params(1),
        name="gmlp_layer",
    )(x, p, gpre, gpost, w_in, ln_g, ln_b, w_mix, b_mix, w_out, w_gate, w_proj)
    return res if emit_v else (res[0], None)


def _fox_proj_kernel(x_ref, gpre_ref, w_ref, wf_ref, wft_ref, bfrow_ref, bfcol_ref,
                     q_ref, kbf_ref, vbf_ref, k_ref, v_ref, sz_ref, logf_ref, *rest,
                     scale, with_cumsum):
    tm, e = k_ref.shape
    nh, dh = q_ref.shape[1], q_ref.shape[3]
    h = _rms_norm(x_ref[...], gpre_ref[...]).astype(BF16)

    q = _dot(h, w_ref[:, 0:e]) * scale
    for hd in range(nh):
        q_ref[0, hd] = q[:, hd * dh:(hd + 1) * dh].astype(BF16)
    k = _dot(h, w_ref[:, e:2 * e])
    k_ref[...] = k
    for hd in range(nh):
        kbf_ref[0, hd] = k[:, hd * dh:(hd + 1) * dh].astype(BF16)
    v = _dot(h, w_ref[:, 2 * e:3 * e])
    v_ref[...] = v
    for hd in range(nh):
        vbf_ref[0, hd] = v[:, hd * dh:(hd + 1) * dh].astype(BF16)
    sz_ref[...] = _silu(_dot(h, w_ref[:, 3 * e:4 * e])).astype(BF16)

    logf = jax.nn.log_sigmoid(_dot(h, wf_ref[...]) + bfrow_ref[...])
    logf_ref[...] = logf
    if with_cumsum:
        c_ref, ct_ref, crow_ref, ccol_ref = rest

        @pl.when(pl.program_id(1) == 0)
        def _():
            crow_ref[...] = jnp.zeros_like(crow_ref)
            ccol_ref[...] = jnp.zeros_like(ccol_ref)

        logf_t = jax.nn.log_sigmoid(
            lax.dot_general(wft_ref[...], h, _NT, preferred_element_type=F32) + bfcol_ref[...])
        c_tile = _dot_ones_lhs(_tri(tm, upper=False), logf) + crow_ref[...]
        ct_tile = _dot_ones_rhs(logf_t, _tri(tm, upper=True)) + ccol_ref[...]
        c_ref[0] = c_tile
        ct_ref[0] = ct_tile
        crow_ref[...] = c_tile[tm - 1:tm, :]
        ccol_ref[...] = ct_tile[:, tm - 1:tm]


def _fox_proj(x, gpre, w, wf, wft, bf_row, bf_col, *, groups, rows, heads, tokens, with_cumsum):
    n, d = x.shape
    e = (w.shape[1]) // 4
    dh = e // heads
    assert n == groups * rows and rows % tokens == 0
    steps = rows // tokens
    tok = lambda b, j: (b * steps + j, 0)
    head_major = lambda b, j: (b, 0, j, 0)
    hm_shape = jax.ShapeDtypeStruct((groups, heads, rows, dh), BF16)
    hm_spec = pl.BlockSpec((1, heads, tokens, dh), head_major)
    out_shape = [hm_shape, hm_shape, hm_shape,
                 jax.ShapeDtypeStruct((n, e), F32), jax.ShapeDtypeStruct((n, e), F32),
                 jax.ShapeDtypeStruct((n, e), BF16), jax.ShapeDtypeStruct((n, heads), F32)]
    out_specs = [hm_spec, hm_spec, hm_spec,
                 pl.BlockSpec((tokens, e), tok), pl.BlockSpec((tokens, e), tok),
                 pl.BlockSpec((tokens, e), tok), pl.BlockSpec((tokens, heads), tok)]
    scratch = []
    if with_cumsum:
        out_shape += [jax.ShapeDtypeStruct((groups, rows, heads), F32),
                      jax.ShapeDtypeStruct((groups, heads, rows), F32)]
        out_specs += [pl.BlockSpec((1, tokens, heads), lambda b, j: (b, j, 0)),
                      pl.BlockSpec((1, heads, tokens), lambda b, j: (b, 0, j))]
        scratch = [pltpu.VMEM((1, heads), F32), pltpu.VMEM((heads, 1), F32)]
    return pl.pallas_call(
        functools.partial(_fox_proj_kernel, scale=float(dh) ** -0.5, with_cumsum=with_cumsum),
        grid=(groups, steps),
        in_specs=[pl.BlockSpec((tokens, d), tok), _resident(gpre.shape), _resident(w.shape),
                  _resident(wf.shape), _resident(wft.shape), _resident(bf_row.shape),
                  _resident(bf_col.shape)],
        out_specs=out_specs,
        out_shape=out_shape,
        scratch_shapes=scratch,
        compiler_params=_params(2),
        name="fox_proj",
    )(x, gpre, w, wf, wft, bf_row, bf_col)


def _fox_attn_kernel(q_ref, k_ref, v_ref, c_ref, ct_ref, o_ref):
    tq, dh = q_ref.shape[2], q_ref.shape[3]
    tk = tq
    hd = pl.program_id(1)
    qi = pl.program_id(2)
    q = q_ref[0, 0]
    c_blk = c_ref[0]
    head = lax.broadcasted_iota(jnp.int32, c_blk.shape, 1)
    cq = jnp.sum(jnp.where(head == hd, c_blk, 0.0), axis=1, keepdims=True)

    def step(j, carry, diagonal):
        m, l, acc = carry
        k0 = pl.multiple_of(j * tk, tk)
        k = k_ref[0, 0, pl.ds(k0, tk), :]
        v = v_ref[0, 0, pl.ds(k0, tk), :]
        ck = ct_ref[0, :, pl.ds(k0, tk)]
        s = lax.dot_general(q, k, _NT, preferred_element_type=F32) + cq - ck
        if diagonal:
            row = lax.broadcasted_iota(jnp.int32, (tq, tk), 0)
            col = lax.broadcasted_iota(jnp.int32, (tq, tk), 1)
            s = jnp.where(col <= row, s, -jnp.inf)
        m_new = jnp.maximum(m, jnp.max(s, axis=1, keepdims=True))
        alpha = jnp.exp(m - m_new)
        p = jnp.exp(s - m_new)
        l = alpha * l + jnp.sum(p, axis=1, keepdims=True)
        acc = alpha * acc + _dot(p.astype(BF16), v)
        return m_new, l, acc

    init = (jnp.full((tq, 1), -jnp.inf, F32), jnp.zeros((tq, 1), F32), jnp.zeros((tq, dh), F32))
    carry = lax.fori_loop(0, qi, lambda j, c: step(j, c, False), init)
    _, l, acc = step(qi, carry, True)
    o_ref[0] = (acc / l).astype(BF16)


def _fox_attn(q, k, v, c, ct):
    b, nh, s, dh = q.shape
    t = ATTN_TILE
    assert s % t == 0
    return pl.pallas_call(
        _fox_attn_kernel,
        grid=(b, nh, s // t),
        in_specs=[
            pl.BlockSpec((1, 1, t, dh), lambda bi, hi, qi: (bi, hi, qi, 0)),
            pl.BlockSpec((1, 1, s, dh), lambda bi, hi, qi: (bi, hi, 0, 0)),
            pl.BlockSpec((1, 1, s, dh), lambda bi, hi, qi: (bi, hi, 0, 0)),
            pl.BlockSpec((1, t, nh), lambda bi, hi, qi: (bi, qi, 0)),
            pl.BlockSpec((1, 1, s), lambda bi, hi, qi: (bi * nh + hi, 0, 0)),
        ],
        out_specs=pl.BlockSpec((1, t, dh), lambda bi, hi, qi: (bi, qi, hi)),
        out_shape=jax.ShapeDtypeStruct((b, s, nh * dh), BF16),
        compiler_params=_params(3),
        name="fox_attn",
    )(q, k, v, c, ct)


def _fox_sample_attn_kernel(q_ref, kc_ref, vc_ref, kn_ref, vn_ref, lfct_ref, lfn_ref, lfnt_ref,
                            o_ref, cct_ref, cnt_ref):
    hps, t, dh = q_ref.shape[1], q_ref.shape[2], q_ref.shape[3]
    nh, past = lfct_ref.shape[1], lfct_ref.shape[2]
    pad = lfnt_ref.shape[2]
    hblk = pl.program_id(1)

    triu = _tri(CUMSUM_CHUNK, upper=True)
    total = jnp.zeros((nh, 1), F32)
    for i in range(past // CUMSUM_CHUNK):
        cs = slice(i * CUMSUM_CHUNK, (i + 1) * CUMSUM_CHUNK)
        blk = _dot_ones_rhs(lfct_ref[0, :, cs], triu) + total
        cct_ref[:, cs] = blk
        total = blk[:, CUMSUM_CHUNK - 1:CUMSUM_CHUNK]
    eye = (lax.broadcasted_iota(jnp.int32, (nh, nh), 0) == lax.broadcasted_iota(jnp.int32, (nh, nh), 1))
    total_row = jnp.sum(jnp.where(eye, total, 0.0), axis=0, keepdims=True)
    cn = _dot_ones_lhs(_tri(t, upper=False), lfn_ref[0]) + total_row
    cnt_ref[...] = _dot_ones_rhs(lfnt_ref[0], _tri(pad, upper=True)) + total
    head = lax.broadcasted_iota(jnp.int32, (t, nh), 1)
    row = lax.broadcasted_iota(jnp.int32, (t, pad), 0)
    col = lax.broadcasted_iota(jnp.int32, (t, pad), 1)
    zeros = jnp.zeros((pad - t, dh), BF16)

    for hh in range(hps):
        hd = hblk * hps + hh
        hs = slice(hh * dh, (hh + 1) * dh)
        q = q_ref[0, hh]
        cq = jnp.sum(jnp.where(head == hd, cn, 0.0), axis=1, keepdims=True)
        kc = kc_ref[0, :, hs].astype(BF16)
        vc = vc_ref[0, :, hs].astype(BF16)
        kn = jnp.concatenate([kn_ref[0, hh], zeros], axis=0)
        vn = jnp.concatenate([vn_ref[0, hh], zeros], axis=0)
        s_c = lax.dot_general(q, kc, _NT, preferred_element_type=F32) + cq - cct_ref[pl.ds(hd, 1), :]
        s_n = lax.dot_general(q, kn, _NT, preferred_element_type=F32) + cq - cnt_ref[pl.ds(hd, 1), :]
        s_n = jnp.where(col <= row, s_n, -jnp.inf)
        m = jnp.maximum(jnp.max(s_c, axis=1, keepdims=True), jnp.max(s_n, axis=1, keepdims=True))
        p_c = jnp.exp(s_c - m)
        p_n = jnp.exp(s_n - m)
        l = jnp.sum(p_c, axis=1, keepdims=True) + jnp.sum(p_n, axis=1, keepdims=True)
        o = _dot(p_c.astype(BF16), vc) + _dot(p_n.astype(BF16), vn)
        o_ref[:, hs] = (o / l).astype(BF16)


def _fox_sample_attn(q, kn, vn, k_cache, v_cache, logf_cache_t, logf_new, logf_new_t, *, batch):
    _, nh, n, dh = q.shape
    t = n // batch
    past = k_cache.shape[1]
    hps = CACHE_HEADS
    assert nh % hps == 0 and past % CUMSUM_CHUNK == 0
    new_spec = pl.BlockSpec((1, hps, t, dh), lambda bi, hi: (0, hi, bi, 0))
    cache_spec = pl.BlockSpec((1, past, hps * dh), lambda bi, hi: (bi, 0, hi))
    whole = lambda shape: pl.BlockSpec((1,) + shape[1:], lambda bi, hi: (bi, 0, 0))
    return pl.pallas_call(
        _fox_sample_attn_kernel,
        grid=(batch, nh // hps),
        in_specs=[new_spec, cache_spec, cache_spec, new_spec, new_spec,
                  whole(logf_cache_t.shape), whole(logf_new.shape), whole(logf_new_t.shape)],
        out_specs=pl.BlockSpec((t, hps * dh), lambda bi, hi: (bi, hi)),
        out_shape=jax.ShapeDtypeStruct((n, nh * dh), BF16),
        scratch_shapes=[pltpu.VMEM((nh, past), F32), pltpu.VMEM((nh, logf_new_t.shape[2]), F32)],
        compiler_params=_params(2),
        name="fox_sample_attn",
    )(q, k_cache, v_cache, kn, vn, logf_cache_t, logf_new, logf_new_t)


def _fox_post_kernel(o_ref, sz_ref, x_ref, p_ref, gpost_ref, wout_ref, wgate_ref, wproj_ref, out_ref):
    o = _dot(o_ref[...] * sz_ref[...], wout_ref[...])
    out_ref[...] = _residual_and_embedding(x_ref[...], o, p_ref[...], gpost_ref, wgate_ref, wproj_ref)


def _fox_post(o, sz, x, p, gpost, w_out, w_gate, w_proj, *, tokens):
    n, d = x.shape
    e = o.shape[1]
    assert n % tokens == 0
    row = lambda i: (i, 0)
    return pl.pallas_call(
        _fox_post_kernel,
        grid=(n // tokens,),
        in_specs=[pl.BlockSpec((tokens, e), row), pl.BlockSpec((tokens, e), row),
                  pl.BlockSpec((tokens, d), row), pl.BlockSpec((tokens, p.shape[1]), row),
                  _resident(gpost.shape), _resident(w_out.shape), _resident(w_gate.shape),
                  _resident(w_proj.shape)],
        out_specs=pl.BlockSpec((tokens, d), row),
        out_shape=jax.ShapeDtypeStruct((n, d), F32),
        compiler_params=_params(1),
        name="fox_post",
    )(o, sz, x, p, gpost, w_out, w_gate, w_proj)


def _mix_operands(w_s, b_s, rows, gd):
    n = min(rows, GMLP_BLOCK)
    reps = GMLP_BLOCK // n
    w = jnp.tile(w_s[:, :n, :n], (1, reps, reps))
    b = jnp.repeat(jnp.tile(b_s[:, :n], (1, reps)).T, gd, axis=1)
    return w, b, n


def kernel(x_prompt, x_sample, cache_fox_k, cache_fox_v, cache_fox_logf, p_prompt, p_sample, norm_pre, norm_post, gmlp_w_in, gmlp_ln_g, gmlp_ln_b, gmlp_w_s, gmlp_b_s, gmlp_w_out, fox_w_in, fox_b_f, fox_w_out, ple_w_proj, ple_w_gate):
    bp, sp, d = x_prompt.shape
    bs, ts, _ = x_sample.shape
    depth = norm_pre.shape[0]
    _, _, past, nh, dh = cache_fox_k.shape
    e = nh * dh
    groups = gmlp_w_s.shape[1]
    gd = e // groups
    n_p, n_s = bp * sp, bs * ts

    xp = x_prompt.reshape(n_p, d)
    xs = x_sample.reshape(n_s, d)
    gmlp_v_s = []
    fk_p, fv_p, flf_p, fk_s, fv_s, flf_s = [], [], [], [], [], []
    for i in range(depth):
        j = i // 2
        gpre, gpost = norm_pre[i][None], norm_post[i][None]
        w_gate, w_proj = ple_w_gate[i].astype(BF16), ple_w_proj[i].astype(BF16)
        pp = p_prompt[i].reshape(n_p, -1)
        ps = p_sample[i].reshape(n_s, -1)
        if i % 2 == 0:
            w_in, w_out = gmlp_w_in[j].astype(BF16), gmlp_w_out[j].astype(BF16)
            ln_g, ln_b = gmlp_ln_g[j][None], gmlp_ln_b[j][None]
            wm_p, bm_p, _ = _mix_operands(gmlp_w_s[j], gmlp_b_s[j], sp, gd)
            wm_s, bm_s, n_mix = _mix_operands(gmlp_w_s[j], gmlp_b_s[j], ts, gd)
            xp, _ = _gmlp_layer(xp, pp, gpre, gpost, w_in, ln_g, ln_b, wm_p, bm_p, w_out, w_gate, w_proj,
                                tokens=GMLP_TOKENS, span=CHUNK, causal=True, emit_v=False)
            assert n_mix <= CHUNK and n_s % GMLP_BLOCK == 0
            xs, v_s = _gmlp_layer(xs, ps, gpre, gpost, w_in, ln_g, ln_b, wm_s, bm_s, w_out, w_gate, w_proj,
                                  tokens=GMLP_BLOCK, span=n_mix, causal=False, emit_v=True)
            gmlp_v_s.append(v_s.reshape(bs, ts, e))
        else:
            w = fox_w_in[j][:, :4 * e].astype(BF16)
            wf = fox_w_in[j][:, 4 * e:].astype(BF16)
            wft = wf.T
            bf_row, bf_col = fox_b_f[j][None], fox_b_f[j][:, None]
            w_out = fox_w_out[j].astype(BF16)

            q, kb, vb, k, v, sz, lf, c, ct = _fox_proj(
                xp, gpre, w, wf, wft, bf_row, bf_col, groups=bp, rows=sp, heads=nh,
                tokens=PROJ_TOKENS, with_cumsum=True)
            o = _fox_attn(q, kb, vb, c, ct.reshape(bp * nh, 1, sp))
            xp = _fox_post(o.reshape(n_p, e), sz, xp, pp, gpost, w_out, w_gate, w_proj, tokens=POST_TOKENS)
            fk_p.append(k.reshape(bp, sp, nh, dh))
            fv_p.append(v.reshape(bp, sp, nh, dh))
            flf_p.append(lf.reshape(bp, sp, nh))

            q, kb, vb, k, v, sz, lf = _fox_proj(
                xs, gpre, w, wf, wft, bf_row, bf_col, groups=1, rows=n_s, heads=nh,
                tokens=n_s, with_cumsum=False)
            lf3 = lf.reshape(bs, ts, nh)
            lf3_t = jnp.pad(lf3.transpose(0, 2, 1), ((0, 0), (0, 0), (0, LANES - ts)))
            o = _fox_sample_attn(
                q, kb, vb, cache_fox_k[j].reshape(bs, past, e), cache_fox_v[j].reshape(bs, past, e),
                cache_fox_logf[j].transpose(0, 2, 1), lf3, lf3_t, batch=bs)
            xs = _fox_post(o, sz, xs, ps, gpost, w_out, w_gate, w_proj, tokens=n_s)
            fk_s.append(k.reshape(bs, ts, nh, dh))
            fv_s.append(v.reshape(bs, ts, nh, dh))
            flf_s.append(lf3)
    return (xp.reshape(bp, sp, d), xs.reshape(bs, ts, d), jnp.stack(gmlp_v_s),
            jnp.stack(fk_p), jnp.stack(fv_p), jnp.stack(flf_p),
            jnp.stack(fk_s), jnp.stack(fv_s), jnp.stack(flf_s))
```

```python
import functools

import jax
import jax.numpy as jnp
from jax import lax
from jax.experimental import pallas as pl
from jax.experimental.pallas import tpu as pltpu

F32 = jnp.float32
BF16 = jnp.bfloat16

CHUNK = 64
GMLP_BLOCK = 128
RMS_EPS = 1e-6
LN_EPS = 1e-5
LANES = 128
VMEM_LIMIT_BYTES = 56 * 1024 * 1024

GMLP_TOKENS = 256
PROJ_TOKENS = 256
POST_TOKENS = 512
ATTN_TILE = 256
ATTN_HEADS = 4
CACHE_CHUNK = 256
LOG2E = 1.4426950408889634
CUMSUM_CHUNK = 256

_NT = (((1,), (1,)), ((), ()))


def _params(n_axes):
    return pltpu.CompilerParams(
        dimension_semantics=("arbitrary",) * n_axes, vmem_limit_bytes=VMEM_LIMIT_BYTES)


def _resident(shape):
    zeros = (0,) * len(shape)
    return pl.BlockSpec(shape, lambda *_: zeros, pipeline_mode=pl.Buffered(1))


def _dot(a, b):
    return jnp.dot(a, b, preferred_element_type=F32)


def _rms_norm(x, g):
    return x * lax.rsqrt(jnp.mean(x * x, axis=-1, keepdims=True) + RMS_EPS) * g


def _silu(z):
    return z * jax.nn.sigmoid(z)


def _split3(x):
    a = x.astype(BF16)
    r = x - a.astype(F32)
    b = r.astype(BF16)
    c = (r - b.astype(F32)).astype(BF16)
    return a, b, c


def _dot_ones_lhs(ones, x):
    a, b, c = _split3(x)
    return (_dot(ones, a) + _dot(ones, b)) + _dot(ones, c)


def _dot_ones_rhs(x, ones):
    a, b, c = _split3(x)
    return (_dot(a, ones) + _dot(b, ones)) + _dot(c, ones)


def _tri(n, upper):
    r = lax.broadcasted_iota(jnp.int32, (n, n), 0)
    c = lax.broadcasted_iota(jnp.int32, (n, n), 1)
    keep = (r <= c) if upper else (c <= r)
    return jnp.where(keep, 1.0, 0.0).astype(BF16)


def _residual_and_embedding(x, o, p, gpost_ref, wgate_ref, wproj_ref):
    x1 = x + _rms_norm(o, gpost_ref[...])
    gate = jax.nn.sigmoid(_dot(x1.astype(BF16), wgate_ref[...]))
    return x1 + gate * _dot(p.astype(BF16), wproj_ref[...])


def _gmlp_layer_kernel(x_ref, p_ref, gpre_ref, gpost_ref, win_ref, lng_ref, lnb_ref,
                       wmix_ref, bmix_ref, wout_ref, wgate_ref, wproj_ref, *rest,
                       span_shift, causal, emit_v):
    if emit_v:
        out_ref, v_ref, y_ref = rest
    else:
        out_ref, y_ref = rest
    tm = x_ref.shape[0]
    e = wout_ref.shape[0]
    groups, blk, _ = wmix_ref.shape
    gd = e // groups

    x = x_ref[...]
    h = _rms_norm(x, gpre_ref[...]).astype(BF16)
    u = jax.nn.gelu(_dot(h, win_ref[:, 0:e]))
    gv = jax.nn.gelu(_dot(h, win_ref[:, e:2 * e]))
    sz = _silu(_dot(h, win_ref[:, 2 * e:3 * e]))
    d = gv - jnp.mean(gv, axis=-1, keepdims=True)
    v = d * lax.rsqrt(jnp.mean(d * d, axis=-1, keepdims=True) + LN_EPS) * lng_ref[...] + lnb_ref[...]
    if emit_v:
        v_ref[...] = v
    vb = v.astype(BF16)

    qs = lax.shift_right_logical(lax.broadcasted_iota(jnp.int32, (blk, blk), 0), span_shift)
    ks = lax.shift_right_logical(lax.broadcasted_iota(jnp.int32, (blk, blk), 1), span_shift)
    allowed = (ks <= qs) if causal else (ks == qs)
    for g in range(groups):
        cs = slice(g * gd, (g + 1) * gd)
        wm = jnp.where(allowed, wmix_ref[g], 0.0).astype(BF16)
        for r in range(tm // blk):
            rs = slice(r * blk, (r + 1) * blk)
            s = _dot(wm, vb[rs, cs]) + bmix_ref[:, cs]
            y_ref[rs, cs] = (u[rs, cs] * s * sz[rs, cs]).astype(BF16)
    o = _dot(y_ref[...], wout_ref[...])
    out_ref[...] = _residual_and_embedding(x, o, p_ref[...], gpost_ref, wgate_ref, wproj_ref)


def _gmlp_layer(x, p, gpre, gpost, w_in, ln_g, ln_b, w_mix, b_mix, w_out, w_gate, w_proj,
                *, tokens, span, causal, emit_v):
    n, d = x.shape
    e = w_out.shape[0]
    pd = p.shape[1]
    assert n % tokens == 0 and tokens % GMLP_BLOCK == 0 and span & (span - 1) == 0
    row = lambda i: (i, 0)
    out_shape = [jax.ShapeDtypeStruct((n, d), F32)]
    out_specs = [pl.BlockSpec((tokens, d), row)]
    if emit_v:
        out_shape.append(jax.ShapeDtypeStruct((n, e), F32))
        out_specs.append(pl.BlockSpec((tokens, e), row))
    res = pl.pallas_call(
        functools.partial(_gmlp_layer_kernel, span_shift=span.bit_length() - 1,
                          causal=causal, emit_v=emit_v),
        grid=(n // tokens,),
        in_specs=[
            pl.BlockSpec((tokens, d), row),
            pl.BlockSpec((tokens, pd), row),
            _resident(gpre.shape), _resident(gpost.shape), _resident(w_in.shape),
            _resident(ln_g.shape), _resident(ln_b.shape), _resident(w_mix.shape),
            _resident(b_mix.shape), _resident(w_out.shape), _resident(w_gate.shape),
            _resident(w_proj.shape),
        ],
        out_specs=out_specs,
        out_shape=out_shape,
        scratch_shapes=[pltpu.VMEM((tokens, e), BF16)],
        compiler_params=_params(1),
        name="gmlp_layer",
    )(x, p, gpre, gpost, w_in, ln_g, ln_b, w_mix, b_mix, w_out, w_gate, w_proj)
    return res if emit_v else (res[0], None)


def _fox_proj_kernel(x_ref, gpre_ref, w_ref, wf_ref, wft_ref, bfrow_ref, bfcol_ref,
                     q_ref, kbf_ref, vbf_ref, k_ref, v_ref, sz_ref, logf_ref, *rest,
                     scale, with_cumsum):
    tm, e = k_ref.shape
    nh, dh = q_ref.shape[1], q_ref.shape[3]
    h = _rms_norm(x_ref[...], gpre_ref[...]).astype(BF16)

    q = _dot(h, w_ref[:, 0:e]) * scale
    for hd in range(nh):
        q_ref[0, hd] = q[:, hd * dh:(hd + 1) * dh].astype(BF16)
    k = _dot(h, w_ref[:, e:2 * e])
    k_ref[...] = k
    for hd in range(nh):
        kbf_ref[0, hd] = k[:, hd * dh:(hd + 1) * dh].astype(BF16)
    v = _dot(h, w_ref[:, 2 * e:3 * e])
    v_ref[...] = v
    for hd in range(nh):
        vbf_ref[0, hd] = v[:, hd * dh:(hd + 1) * dh].astype(BF16)
    sz_ref[...] = _silu(_dot(h, w_ref[:, 3 * e:4 * e])).astype(BF16)

    logf = jax.nn.log_sigmoid(_dot(h, wf_ref[...]) + bfrow_ref[...])
    logf_ref[...] = logf
    if with_cumsum:
        c_ref, ct_ref, crow_ref, ccol_ref = rest

        @pl.when(pl.program_id(1) == 0)
        def _():
            crow_ref[...] = jnp.zeros_like(crow_ref)
            ccol_ref[...] = jnp.zeros_like(ccol_ref)

        logf_t = jax.nn.log_sigmoid(
            lax.dot_general(wft_ref[...], h, _NT, preferred_element_type=F32) + bfcol_ref[...])
        c_tile = _dot_ones_lhs(_tri(tm, upper=False), logf) + crow_ref[...]
        ct_tile = _dot_ones_rhs(logf_t, _tri(tm, upper=True)) + ccol_ref[...]
        c_ref[0] = c_tile * LOG2E
        ct_ref[0] = ct_tile * LOG2E
        crow_ref[...] = c_tile[tm - 1:tm, :]
        ccol_ref[...] = ct_tile[:, tm - 1:tm]


def _fox_proj(x, gpre, w, wf, wft, bf_row, bf_col, *, groups, rows, heads, tokens, with_cumsum):
    n, d = x.shape
    e = (w.shape[1]) // 4
    dh = e // heads
    assert n == groups * rows and rows % tokens == 0
    steps = rows // tokens
    tok = lambda b, j: (b * steps + j, 0)
    head_major = lambda b, j: (b, 0, j, 0)
    hm_shape = jax.ShapeDtypeStruct((groups, heads, rows, dh), BF16)
    hm_spec = pl.BlockSpec((1, heads, tokens, dh), head_major)
    out_shape = [hm_shape, hm_shape, hm_shape,
                 jax.ShapeDtypeStruct((n, e), F32), jax.ShapeDtypeStruct((n, e), F32),
                 jax.ShapeDtypeStruct((n, e), BF16), jax.ShapeDtypeStruct((n, heads), F32)]
    out_specs = [hm_spec, hm_spec, hm_spec,
                 pl.BlockSpec((tokens, e), tok), pl.BlockSpec((tokens, e), tok),
                 pl.BlockSpec((tokens, e), tok), pl.BlockSpec((tokens, heads), tok)]
    scratch = []
    if with_cumsum:
        out_shape += [jax.ShapeDtypeStruct((groups, rows, heads), F32),
                      jax.ShapeDtypeStruct((groups, heads, rows), F32)]
        out_specs += [pl.BlockSpec((1, tokens, heads), lambda b, j: (b, j, 0)),
                      pl.BlockSpec((1, heads, tokens), lambda b, j: (b, 0, j))]
        scratch = [pltpu.VMEM((1, heads), F32), pltpu.VMEM((heads, 1), F32)]
    return pl.pallas_call(
        functools.partial(_fox_proj_kernel, scale=float(dh) ** -0.5 * (LOG2E if with_cumsum else 1.0),
                          with_cumsum=with_cumsum),
        grid=(groups, steps),
        in_specs=[pl.BlockSpec((tokens, d), tok), _resident(gpre.shape), _resident(w.shape),
                  _resident(wf.shape), _resident(wft.shape), _resident(bf_row.shape),
                  _resident(bf_col.shape)],
        out_specs=out_specs,
        out_shape=out_shape,
        scratch_shapes=scratch,
        compiler_params=_params(2),
        name="fox_proj",
    )(x, gpre, w, wf, wft, bf_row, bf_col)


def _fox_attn_kernel(q_ref, k_ref, v_ref, c_ref, ct_ref, o_ref):
    hpb, tq, dh = q_ref.shape[1], q_ref.shape[2], q_ref.shape[3]
    tk = 2 * tq
    hblk = pl.program_id(1)
    qi = pl.program_id(2)
    c_blk = c_ref[0]
    head = lax.broadcasted_iota(jnp.int32, c_blk.shape, 1)
    qs = [q_ref[0, hh] for hh in range(hpb)]
    cqs = [jnp.sum(jnp.where(head == hblk * hpb + hh, c_blk, 0.0), axis=1, keepdims=True)
           for hh in range(hpb)]

    def step(j, carry, diagonal):
        k0 = pl.multiple_of(j * tk, tk)
        if diagonal:
            row = lax.broadcasted_iota(jnp.int32, (tq, tk), 0) + qi * tq
            col = lax.broadcasted_iota(jnp.int32, (tq, tk), 1) + k0
            visible = col <= row
        out = []
        for hh in range(hpb):
            m, l, acc = carry[hh]
            k = k_ref[0, hh, pl.ds(k0, tk), :]
            v = v_ref[0, hh, pl.ds(k0, tk), :]
            ck = ct_ref[0, hh:hh + 1, pl.ds(k0, tk)]
            s = lax.dot_general(qs[hh], k, _NT, preferred_element_type=F32) + cqs[hh] - ck
            if diagonal:
                s = jnp.where(visible, s, -jnp.inf)
            m_new = jnp.maximum(m, jnp.max(s, axis=1, keepdims=True))
            alpha = jnp.exp2(m - m_new)
            p = jnp.exp2(s - m_new)
            part = p[:, 0:LANES]
            for t in range(1, tk // LANES):
                part = part + p[:, t * LANES:(t + 1) * LANES]
            out.append((m_new, alpha * l + part, alpha * acc + _dot(p.astype(BF16), v)))
        return tuple(out)

    init = tuple((jnp.full((tq, 1), -jnp.inf, F32), jnp.zeros((tq, LANES), F32),
                  jnp.zeros((tq, dh), F32)) for _ in range(hpb))
    n_full = lax.shift_right_logical(qi, 1)
    carry = lax.fori_loop(0, n_full, lambda j, c: step(j, c, False), init)
    carry = step(n_full, carry, True)
    for hh in range(hpb):
        _, l, acc = carry[hh]
        o_ref[0, :, hh * dh:(hh + 1) * dh] = (acc / jnp.sum(l, axis=1, keepdims=True)).astype(BF16)


def _fox_attn(q, k, v, c, ct):
    b, nh, s, dh = q.shape
    t, hpb = ATTN_TILE, ATTN_HEADS
    assert s % (2 * t) == 0 and nh % hpb == 0 and dh == LANES
    nhb = nh // hpb
    ct = ct.reshape(b * nhb, hpb, s)
    return pl.pallas_call(
        _fox_attn_kernel,
        grid=(b, nhb, s // t),
        in_specs=[
            pl.BlockSpec((1, hpb, t, dh), lambda bi, hi, qi: (bi, hi, qi, 0)),
            pl.BlockSpec((1, hpb, s, dh), lambda bi, hi, qi: (bi, hi, 0, 0)),
            pl.BlockSpec((1, hpb, s, dh), lambda bi, hi, qi: (bi, hi, 0, 0)),
            pl.BlockSpec((1, t, nh), lambda bi, hi, qi: (bi, qi, 0)),
            pl.BlockSpec((1, hpb, s), lambda bi, hi, qi: (bi * nhb + hi, 0, 0)),
        ],
        out_specs=pl.BlockSpec((1, t, hpb * dh), lambda bi, hi, qi: (bi, qi, hi)),
        out_shape=jax.ShapeDtypeStruct((b, s, nh * dh), BF16),
        compiler_params=_params(3),
        name="fox_attn",
    )(q, k, v, c, ct)


def _fox_sample_attn_kernel(q_ref, kc_ref, vc_ref, kn_ref, vn_ref, lfct_ref, lfn_ref, lfnt_ref,
                            o_ref, cct_ref, cnt_ref, cn_ref, m_ref, l_ref, acc_ref):
    nh, t, dh = q_ref.shape[1], q_ref.shape[2], q_ref.shape[3]
    past = lfct_ref.shape[2]
    pad = lfnt_ref.shape[2]
    pc = kc_ref.shape[2] // nh
    step = pl.program_id(1)

    @pl.when(step == 0)
    def _():
        triu = _tri(CUMSUM_CHUNK, upper=True)
        total = jnp.zeros((nh, 1), F32)
        for i in range(past // CUMSUM_CHUNK):
            cs = slice(i * CUMSUM_CHUNK, (i + 1) * CUMSUM_CHUNK)
            blk = _dot_ones_rhs(lfct_ref[0, :, cs], triu) + total
            cct_ref[:, cs] = blk
            total = blk[:, CUMSUM_CHUNK - 1:CUMSUM_CHUNK]
        eye = (lax.broadcasted_iota(jnp.int32, (nh, nh), 0)
               == lax.broadcasted_iota(jnp.int32, (nh, nh), 1))
        total_row = jnp.sum(jnp.where(eye, total, 0.0), axis=0, keepdims=True)
        cn_ref[...] = _dot_ones_lhs(_tri(t, upper=False), lfn_ref[0]) + total_row
        cnt_ref[...] = _dot_ones_rhs(lfnt_ref[0], _tri(pad, upper=True)) + total
        m_ref[...] = jnp.full_like(m_ref, -jnp.inf)
        l_ref[...] = jnp.zeros_like(l_ref)
        acc_ref[...] = jnp.zeros_like(acc_ref)

    def update(hd, s, v):
        m = m_ref[hd]
        m_new = jnp.maximum(m, jnp.max(s, axis=1, keepdims=True))
        alpha = jnp.exp(m - m_new)
        p = jnp.exp(s - m_new)
        m_ref[hd] = m_new
        l_ref[hd] = alpha * l_ref[hd] + jnp.sum(p, axis=1, keepdims=True)
        acc_ref[hd] = alpha * acc_ref[hd] + _dot(p.astype(BF16), v)

    p0 = pl.multiple_of(step * pc, pc)
    for hd in range(nh):
        kc = kc_ref.at[0, 0][pl.ds(hd, pc, stride=nh), :].astype(BF16)
        vc = vc_ref.at[0, 0][pl.ds(hd, pc, stride=nh), :].astype(BF16)
        s = (lax.dot_general(q_ref[0, hd], kc, _NT, preferred_element_type=F32)
             + cn_ref[:, hd:hd + 1] - cct_ref[hd:hd + 1, pl.ds(p0, pc)])
        update(hd, s, vc)

    @pl.when(step == pl.num_programs(1) - 1)
    def _():
        row = lax.broadcasted_iota(jnp.int32, (t, pad), 0)
        col = lax.broadcasted_iota(jnp.int32, (t, pad), 1)
        zeros = jnp.zeros((pad - t, dh), BF16)
        for hd in range(nh):
            kn = jnp.concatenate([kn_ref[0, hd], zeros], axis=0)
            vn = jnp.concatenate([vn_ref[0, hd], zeros], axis=0)
            s = (lax.dot_general(q_ref[0, hd], kn, _NT, preferred_element_type=F32)
                 + cn_ref[:, hd:hd + 1] - cnt_ref[hd:hd + 1, :])
            update(hd, jnp.where(col <= row, s, -jnp.inf), vn)
            o_ref[:, hd * dh:(hd + 1) * dh] = (acc_ref[hd] / l_ref[hd]).astype(BF16)


def _fox_sample_attn(q, kn, vn, k_cache, v_cache, layer, logf_cache_t, logf_new, logf_new_t, *, batch):
    _, nh, n, dh = q.shape
    t = n // batch
    past = k_cache.shape[2] // nh
    pc = CACHE_CHUNK
    assert past % pc == 0 and past % CUMSUM_CHUNK == 0 and dh == LANES
    new_spec = pl.BlockSpec((1, nh, t, dh), lambda bi, si: (0, 0, bi, 0))
    cache_spec = pl.BlockSpec((1, 1, pc * nh, dh), lambda bi, si: (layer, bi, si, 0))
    whole = lambda shape: pl.BlockSpec((1,) + shape[1:], lambda bi, si: (bi, 0, 0))
    pad = logf_new_t.shape[2]
    return pl.pallas_call(
        _fox_sample_attn_kernel,
        grid=(batch, past // pc),
        in_specs=[new_spec, cache_spec, cache_spec, new_spec, new_spec,
                  whole(logf_cache_t.shape), whole(logf_new.shape), whole(logf_new_t.shape)],
        out_specs=pl.BlockSpec((t, nh * dh), lambda bi, si: (bi, 0)),
        out_shape=jax.ShapeDtypeStruct((n, nh * dh), BF16),
        scratch_shapes=[pltpu.VMEM((nh, past), F32), pltpu.VMEM((nh, pad), F32),
                        pltpu.VMEM((t, nh), F32), pltpu.VMEM((nh, t, 1), F32),
                        pltpu.VMEM((nh, t, 1), F32), pltpu.VMEM((nh, t, dh), F32)],
        compiler_params=_params(2),
        name="fox_sample_attn",
    )(q, k_cache, v_cache, kn, vn, logf_cache_t, logf_new, logf_new_t)


def _fox_post_kernel(o_ref, sz_ref, x_ref, p_ref, gpost_ref, wout_ref, wgate_ref, wproj_ref, out_ref):
    o = _dot(o_ref[...] * sz_ref[...], wout_ref[...])
    out_ref[...] = _residual_and_embedding(x_ref[...], o, p_ref[...], gpost_ref, wgate_ref, wproj_ref)


def _fox_post(o, sz, x, p, gpost, w_out, w_gate, w_proj, *, tokens):
    n, d = x.shape
    e = o.shape[1]
    assert n % tokens == 0
    row = lambda i: (i, 0)
    return pl.pallas_call(
        _fox_post_kernel,
        grid=(n // tokens,),
        in_specs=[pl.BlockSpec((tokens, e), row), pl.BlockSpec((tokens, e), row),
                  pl.BlockSpec((tokens, d), row), pl.BlockSpec((tokens, p.shape[1]), row),
                  _resident(gpost.shape), _resident(w_out.shape), _resident(w_gate.shape),
                  _resident(w_proj.shape)],
        out_specs=pl.BlockSpec((tokens, d), row),
        out_shape=jax.ShapeDtypeStruct((n, d), F32),
        compiler_params=_params(1),
        name="fox_post",
    )(o, sz, x, p, gpost, w_out, w_gate, w_proj)


def _mix_operands(w_s, b_s, rows, gd):
    n = min(rows, GMLP_BLOCK)
    reps = GMLP_BLOCK // n
    w = jnp.tile(w_s[:, :n, :n], (1, reps, reps))
    b = jnp.repeat(jnp.tile(b_s[:, :n], (1, reps)).T, gd, axis=1)
    return w, b, n


def kernel(x_prompt, x_sample, cache_fox_k, cache_fox_v, cache_fox_logf, p_prompt, p_sample, norm_pre, norm_post, gmlp_w_in, gmlp_ln_g, gmlp_ln_b, gmlp_w_s, gmlp_b_s, gmlp_w_out, fox_w_in, fox_b_f, fox_w_out, ple_w_proj, ple_w_gate):
    bp, sp, d = x_prompt.shape
    bs, ts, _ = x_sample.shape
    depth = norm_pre.shape[0]
    _, _, past, nh, dh = cache_fox_k.shape
    e = nh * dh
    groups = gmlp_w_s.shape[1]
    gd = e // groups
    n_p, n_s = bp * sp, bs * ts

    xp = x_prompt.reshape(n_p, d)
    xs = x_sample.reshape(n_s, d)
    cache_k = cache_fox_k.reshape(cache_fox_k.shape[0], bs, past * nh, dh)
    cache_v = cache_fox_v.reshape(cache_fox_v.shape[0], bs, past * nh, dh)
    gmlp_v_s = []
    fk_p, fv_p, flf_p, fk_s, fv_s, flf_s = [], [], [], [], [], []
    for i in range(depth):
        j = i // 2
        gpre, gpost = norm_pre[i][None], norm_post[i][None]
        w_gate, w_proj = ple_w_gate[i].astype(BF16), ple_w_proj[i].astype(BF16)
        pp = p_prompt[i].reshape(n_p, -1)
        ps = p_sample[i].reshape(n_s, -1)
        if i % 2 == 0:
            w_in, w_out = gmlp_w_in[j].astype(BF16), gmlp_w_out[j].astype(BF16)
            ln_g, ln_b = gmlp_ln_g[j][None], gmlp_ln_b[j][None]
            wm_p, bm_p, _ = _mix_operands(gmlp_w_s[j], gmlp_b_s[j], sp, gd)
            wm_s, bm_s, n_mix = _mix_operands(gmlp_w_s[j], gmlp_b_s[j], ts, gd)
            xp, _ = _gmlp_layer(xp, pp, gpre, gpost, w_in, ln_g, ln_b, wm_p, bm_p, w_out, w_gate, w_proj,
                                tokens=GMLP_TOKENS, span=CHUNK, causal=True, emit_v=False)
            assert n_mix <= CHUNK and n_s % GMLP_BLOCK == 0
            xs, v_s = _gmlp_layer(xs, ps, gpre, gpost, w_in, ln_g, ln_b, wm_s, bm_s, w_out, w_gate, w_proj,
                                  tokens=GMLP_BLOCK, span=n_mix, causal=False, emit_v=True)
            gmlp_v_s.append(v_s.reshape(bs, ts, e))
        else:
            w = fox_w_in[j][:, :4 * e].astype(BF16)
            wf = fox_w_in[j][:, 4 * e:].astype(BF16)
            wft = wf.T
            bf_row, bf_col = fox_b_f[j][None], fox_b_f[j][:, None]
            w_out = fox_w_out[j].astype(BF16)

            q, kb, vb, k, v, sz, lf, c, ct = _fox_proj(
                xp, gpre, w, wf, wft, bf_row, bf_col, groups=bp, rows=sp, heads=nh,
                tokens=PROJ_TOKENS, with_cumsum=True)
            o = _fox_attn(q, kb, vb, c, ct)
            xp = _fox_post(o.reshape(n_p, e), sz, xp, pp, gpost, w_out, w_gate, w_proj, tokens=POST_TOKENS)
            fk_p.append(k.reshape(bp, sp, nh, dh))
            fv_p.append(v.reshape(bp, sp, nh, dh))
            flf_p.append(lf.reshape(bp, sp, nh))

            q, kb, vb, k, v, sz, lf = _fox_proj(
                xs, gpre, w, wf, wft, bf_row, bf_col, groups=1, rows=n_s, heads=nh,
                tokens=n_s, with_cumsum=False)
            lf3 = lf.reshape(bs, ts, nh)
            lf3_t = jnp.pad(lf3.transpose(0, 2, 1), ((0, 0), (0, 0), (0, LANES - ts)))
            o = _fox_sample_attn(q, kb, vb, cache_k, cache_v, j, cache_fox_logf[j].transpose(0, 2, 1),
                                 lf3, lf3_t, batch=bs)
            xs = _fox_post(o, sz, xs, ps, gpost, w_out, w_gate, w_proj, tokens=n_s)
            fk_s.append(k.reshape(bs, ts, nh, dh))
            fv_s.append(v.reshape(bs, ts, nh, dh))
            flf_s.append(lf3)
    return (xp.reshape(bp, sp, d), xs.reshape(bs, ts, d), jnp.stack(gmlp_v_s),
            jnp.stack(fk_p), jnp.stack(fv_p), jnp.stack(flf_p),
            jnp.stack(fk_s), jnp.stack(fv_s), jnp.stack(flf_s))
```

```python
import functools

import jax
import jax.numpy as jnp
from jax import lax
from jax.experimental import pallas as pl
from jax.experimental.pallas import tpu as pltpu

F32 = jnp.float32
BF16 = jnp.bfloat16

CHUNK = 64
GMLP_BLOCK = 128
RMS_EPS = 1e-6
LN_EPS = 1e-5
LANES = 128
VMEM_LIMIT_BYTES = 56 * 1024 * 1024

GMLP_TOKENS = 256
PROJ_TOKENS = 256
POST_TOKENS = 512
ATTN_TILE = 256
ATTN_HEADS = 4
CACHE_CHUNK = 256
LOG2E = 1.4426950408889634
CUMSUM_CHUNK = 256

_NT = (((1,), (1,)), ((), ()))


def _params(n_axes):
    return pltpu.CompilerParams(
        dimension_semantics=("arbitrary",) * n_axes, vmem_limit_bytes=VMEM_LIMIT_BYTES)


def _resident(shape):
    zeros = (0,) * len(shape)
    return pl.BlockSpec(shape, lambda *_: zeros, pipeline_mode=pl.Buffered(1))


def _dot(a, b):
    return jnp.dot(a, b, preferred_element_type=F32)


def _rms_norm(x, g):
    return x * lax.rsqrt(jnp.mean(x * x, axis=-1, keepdims=True) + RMS_EPS) * g


def _silu(z):
    return z * jax.nn.sigmoid(z)


def _split3(x):
    a = x.astype(BF16)
    r = x - a.astype(F32)
    b = r.astype(BF16)
    c = (r - b.astype(F32)).astype(BF16)
    return a, b, c


def _dot_ones_lhs(ones, x):
    a, b, c = _split3(x)
    return (_dot(ones, a) + _dot(ones, b)) + _dot(ones, c)


def _dot_ones_rhs(x, ones):
    a, b, c = _split3(x)
    return (_dot(a, ones) + _dot(b, ones)) + _dot(c, ones)


def _tri(n, upper):
    r = lax.broadcasted_iota(jnp.int32, (n, n), 0)
    c = lax.broadcasted_iota(jnp.int32, (n, n), 1)
    keep = (r <= c) if upper else (c <= r)
    return jnp.where(keep, 1.0, 0.0).astype(BF16)


def _residual_and_embedding(x, o, p, gpost_ref, wgate_ref, wproj_ref):
    x1 = x + _rms_norm(o, gpost_ref[...])
    gate = jax.nn.sigmoid(_dot(x1.astype(BF16), wgate_ref[...]))
    return x1 + gate * _dot(p.astype(BF16), wproj_ref[...])


def _gmlp_layer_kernel(x_ref, p_ref, gpre_ref, gpost_ref, win_ref, lng_ref, lnb_ref,
                       wmix_ref, bmix_ref, wout_ref, wgate_ref, wproj_ref, *rest,
                       span_shift, causal, emit_v):
    if emit_v:
        out_ref, v_ref, y_ref = rest
    else:
        out_ref, y_ref = rest
    tm = x_ref.shape[0]
    e = wout_ref.shape[0]
    groups, blk, _ = wmix_ref.shape
    gd = e // groups

    x = x_ref[...]
    h = _rms_norm(x, gpre_ref[...]).astype(BF16)
    u = jax.nn.gelu(_dot(h, win_ref[:, 0:e]))
    gv = jax.nn.gelu(_dot(h, win_ref[:, e:2 * e]))
    sz = _silu(_dot(h, win_ref[:, 2 * e:3 * e]))
    d = gv - jnp.mean(gv, axis=-1, keepdims=True)
    v = d * lax.rsqrt(jnp.mean(d * d, axis=-1, keepdims=True) + LN_EPS) * lng_ref[...] + lnb_ref[...]
    if emit_v:
        v_ref[...] = v
    vb = v.astype(BF16)

    qs = lax.shift_right_logical(lax.broadcasted_iota(jnp.int32, (blk, blk), 0), span_shift)
    ks = lax.shift_right_logical(lax.broadcasted_iota(jnp.int32, (blk, blk), 1), span_shift)
    allowed = (ks <= qs) if causal else (ks == qs)
    for g in range(groups):
        cs = slice(g * gd, (g + 1) * gd)
        wm = jnp.where(allowed, wmix_ref[g], 0.0).astype(BF16)
        for r in range(tm // blk):
            rs = slice(r * blk, (r + 1) * blk)
            s = _dot(wm, vb[rs, cs]) + bmix_ref[:, cs]
            y_ref[rs, cs] = (u[rs, cs] * s * sz[rs, cs]).astype(BF16)
    o = _dot(y_ref[...], wout_ref[...])
    out_ref[...] = _residual_and_embedding(x, o, p_ref[...], gpost_ref, wgate_ref, wproj_ref)


def _gmlp_layer(x, p, gpre, gpost, w_in, ln_g, ln_b, w_mix, b_mix, w_out, w_gate, w_proj,
                *, tokens, span, causal, emit_v):
    n, d = x.shape
    e = w_out.shape[0]
    pd = p.shape[1]
    assert n % tokens == 0 and tokens % GMLP_BLOCK == 0 and span & (span - 1) == 0
    row = lambda i: (i, 0)
    out_shape = [jax.ShapeDtypeStruct((n, d), F32)]
    out_specs = [pl.BlockSpec((tokens, d), row)]
    if emit_v:
        out_shape.append(jax.ShapeDtypeStruct((n, e), F32))
        out_specs.append(pl.BlockSpec((tokens, e), row))
    res = pl.pallas_call(
        functools.partial(_gmlp_layer_kernel, span_shift=span.bit_length() - 1,
                          causal=causal, emit_v=emit_v),
        grid=(n // tokens,),
        in_specs=[
            pl.BlockSpec((tokens, d), row),
            pl.BlockSpec((tokens, pd), row),
            _resident(gpre.shape), _resident(gpost.shape), _resident(w_in.shape),
            _resident(ln_g.shape), _resident(ln_b.shape), _resident(w_mix.shape),
            _resident(b_mix.shape), _resident(w_out.shape), _resident(w_gate.shape),
            _resident(w_proj.shape),
        ],
        out_specs=out_specs,
        out_shape=out_shape,
        scratch_shapes=[pltpu.VMEM((tokens, e), BF16)],
        compiler_params=_params(1),
        name="gmlp_layer",
    )(x, p, gpre, gpost, w_in, ln_g, ln_b, w_mix, b_mix, w_out, w_gate, w_proj)
    return res if emit_v else (res[0], None)


def _bias_columns(c, hd, width):
    hi, mid, lo = _split3(c[:, hd:hd + 1])
    shape = (c.shape[0], width)
    return tuple(jnp.broadcast_to(t.astype(F32), shape) for t in (hi, mid, lo))


def _fox_proj_kernel(x_ref, gpre_ref, w_ref, wf_ref, bf_ref, kf_in_ref, vf_in_ref,
                     q_ref, k_ref, v_ref, kf_ref, vf_ref, sz_ref, logf_ref, *scratch, scale, folded_bias):
    del kf_in_ref, vf_in_ref
    tm, e = sz_ref.shape
    nh = q_ref.shape[1]
    dh = e // nh
    h = _rms_norm(x_ref[...], gpre_ref[...]).astype(BF16)
    logf = jax.nn.log_sigmoid(_dot(h, wf_ref[...]) + bf_ref[...])
    logf_ref[...] = logf

    q = _dot(h, w_ref[:, 0:e]) * scale
    k = _dot(h, w_ref[:, e:2 * e])
    v = _dot(h, w_ref[:, 2 * e:3 * e])
    for hd in range(nh):
        hs = slice(hd * dh, (hd + 1) * dh)
        kf_ref.at[0, 0][pl.ds(hd, tm, stride=nh), :] = k[:, hs]
        vf_ref.at[0, 0][pl.ds(hd, tm, stride=nh), :] = v[:, hs]
    sz_ref[...] = _silu(_dot(h, w_ref[:, 3 * e:4 * e])).astype(BF16)

    if not folded_bias:
        for hd in range(nh):
            hs = slice(hd * dh, (hd + 1) * dh)
            q_ref[0, hd] = q[:, hs].astype(BF16)
            k_ref[0, hd] = k[:, hs].astype(BF16)
            v_ref[0, hd] = v[:, hs].astype(BF16)
        return

    (carry_ref,) = scratch

    @pl.when(pl.program_id(1) == 0)
    def _():
        carry_ref[...] = jnp.zeros_like(carry_ref)

    c = _dot_ones_lhs(_tri(tm, upper=False), logf) + carry_ref[...]
    carry_ref[...] = c[tm - 1:tm, :]
    c = c * LOG2E
    lane = lax.broadcasted_iota(jnp.int32, (tm, dh), 1)
    ones_q = jnp.where((lane >= 3) & (lane < 6), 1.0, 0.0)
    ones_k = jnp.where(lane < 3, 1.0, 0.0)
    for hd in range(nh):
        hs = slice(hd * dh, (hd + 1) * dh)
        hi, mid, lo = _bias_columns(c, hd, dh)
        aux_q = jnp.where(lane == 0, hi, jnp.where(lane == 1, mid, jnp.where(lane == 2, lo, ones_q)))
        aux_k = jnp.where(lane == 3, -hi, jnp.where(lane == 4, -mid, jnp.where(lane == 5, -lo, ones_k)))
        q_ref[0, hd, :, 0:dh] = q[:, hs].astype(BF16)
        q_ref[0, hd, :, dh:2 * dh] = aux_q.astype(BF16)
        k_ref[0, hd, :, 0:dh] = k[:, hs].astype(BF16)
        k_ref[0, hd, :, dh:2 * dh] = aux_k.astype(BF16)
        v_ref[0, hd] = v[:, hs].T.astype(BF16)


def _fox_proj(x, gpre, w, wf, bf, kv_leaves, *, layer, groups, rows, heads, tokens, folded_bias):
    n, d = x.shape
    e = (w.shape[1]) // 4
    dh = e // heads
    assert n == groups * rows and rows % tokens == 0
    steps = rows // tokens
    tok = lambda b, j: (b * steps + j, 0)
    if folded_bias:
        qk_shape = jax.ShapeDtypeStruct((groups, heads, rows, 2 * dh), BF16)
        qk_spec = pl.BlockSpec((1, heads, tokens, 2 * dh), lambda b, j: (b, 0, j, 0))
        v_shape = jax.ShapeDtypeStruct((groups, heads, dh, rows), BF16)
        v_spec = pl.BlockSpec((1, heads, dh, tokens), lambda b, j: (b, 0, 0, j))
        scratch = [pltpu.VMEM((1, heads), F32)]
        scale = float(dh) ** -0.5 * LOG2E
    else:
        qk_shape = v_shape = jax.ShapeDtypeStruct((groups, heads, rows, dh), BF16)
        qk_spec = v_spec = pl.BlockSpec((1, heads, tokens, dh), lambda b, j: (b, 0, j, 0))
        scratch = []
        scale = float(dh) ** -0.5
    k_leaf, v_leaf = kv_leaves
    assert k_leaf.shape == v_leaf.shape == (k_leaf.shape[0], groups, rows * heads, dh)
    leaf_shape = jax.ShapeDtypeStruct(k_leaf.shape, F32)
    leaf_spec = pl.BlockSpec((1, 1, tokens * heads, dh), lambda b, j: (layer, b, j, 0))
    return pl.pallas_call(
        functools.partial(_fox_proj_kernel, scale=scale, folded_bias=folded_bias),
        grid=(groups, steps),
        in_specs=[pl.BlockSpec((tokens, d), tok), _resident(gpre.shape), _resident(w.shape),
                  _resident(wf.shape), _resident(bf.shape),
                  pl.BlockSpec(memory_space=pl.ANY), pl.BlockSpec(memory_space=pl.ANY)],
        out_specs=[qk_spec, qk_spec, v_spec, leaf_spec, leaf_spec,
                   pl.BlockSpec((tokens, e), tok), pl.BlockSpec((tokens, heads), tok)],
        out_shape=[qk_shape, qk_shape, v_shape, leaf_shape, leaf_shape,
                   jax.ShapeDtypeStruct((n, e), BF16), jax.ShapeDtypeStruct((n, heads), F32)],
        input_output_aliases={5: 3, 6: 4},
        scratch_shapes=scratch,
        compiler_params=_params(2),
        name="fox_proj",
    )(x, gpre, w, wf, bf, k_leaf, v_leaf)


def _fox_attn_kernel(q_ref, k_ref, vt_ref, o_ref, s_ref, smax_ref, m_ref, l_ref, acc_ref):
    hpb, tq = q_ref.shape[1], q_ref.shape[2]
    tk = s_ref.shape[1]
    qi = pl.program_id(2)

    def produce(j):
        k0 = pl.multiple_of(j * tk, tk)
        for hh in range(hpb):
            s = lax.dot_general(k_ref[0, hh, pl.ds(k0, tk), :], q_ref[0, hh], _NT,
                                preferred_element_type=F32)
            s_ref[hh] = s
            smax_ref[hh] = jnp.max(s, axis=0, keepdims=True)

    def consume(j, diagonal):
        k0 = pl.multiple_of(j * tk, tk)
        if diagonal:
            key = lax.broadcasted_iota(jnp.int32, (tk, tq), 0) + k0
            query = lax.broadcasted_iota(jnp.int32, (tk, tq), 1) + qi * tq
            visible = key <= query
        for hh in range(hpb):
            s = s_ref[hh]
            if diagonal:
                s = jnp.where(visible, s, -jnp.inf)
                s_max = jnp.max(s, axis=0, keepdims=True)
            else:
                s_max = smax_ref[hh]
            m = m_ref[hh]
            m_new = jnp.maximum(m, s_max)
            alpha = jnp.exp2(m - m_new)
            p = jnp.exp2(s - m_new)
            m_ref[hh] = m_new
            l_ref[hh] = alpha * l_ref[hh] + jnp.sum(p, axis=0, keepdims=True)
            acc_ref[hh] = alpha * acc_ref[hh] + _dot(vt_ref[0, hh, :, pl.ds(k0, tk)], p.astype(BF16))

    m_ref[...] = jnp.full_like(m_ref, -jnp.inf)
    l_ref[...] = jnp.zeros_like(l_ref)
    acc_ref[...] = jnp.zeros_like(acc_ref)
    n_full = lax.shift_right_logical(qi, 1)
    produce(0)

    @pl.loop(0, n_full)
    def _(j):
        consume(j, False)
        produce(j + 1)

    consume(n_full, True)
    dh = acc_ref.shape[1]
    for hh in range(hpb):
        o_ref[0, :, hh * dh:(hh + 1) * dh] = (acc_ref[hh] / l_ref[hh]).T.astype(BF16)


def _fox_attn(q, k, vt):
    b, nh, s, dh2 = q.shape
    dh = vt.shape[2]
    t, hpb = ATTN_TILE, ATTN_HEADS
    assert s % (2 * t) == 0 and nh % hpb == 0 and dh == LANES and dh2 == 2 * dh
    return pl.pallas_call(
        _fox_attn_kernel,
        grid=(b, nh // hpb, s // t),
        in_specs=[
            pl.BlockSpec((1, hpb, t, dh2), lambda bi, hi, qi: (bi, hi, qi, 0)),
            pl.BlockSpec((1, hpb, s, dh2), lambda bi, hi, qi: (bi, hi, 0, 0)),
            pl.BlockSpec((1, hpb, dh, s), lambda bi, hi, qi: (bi, hi, 0, 0)),
        ],
        out_specs=pl.BlockSpec((1, t, hpb * dh), lambda bi, hi, qi: (bi, qi, hi)),
        out_shape=jax.ShapeDtypeStruct((b, s, nh * dh), BF16),
        scratch_shapes=[pltpu.VMEM((hpb, 2 * t, t), F32), pltpu.VMEM((hpb, 1, t), F32),
                        pltpu.VMEM((hpb, 1, t), F32), pltpu.VMEM((hpb, 1, t), F32),
                        pltpu.VMEM((hpb, dh, t), F32)],
        compiler_params=_params(3),
        name="fox_attn",
    )(q, k, vt)


def _fox_sample_attn_kernel(q_ref, kc_ref, vc_ref, kn_ref, vn_ref, lfct_ref, lfn_ref, lfnt_ref,
                            o_ref, cct_ref, cnt_ref, cn_ref, m_ref, l_ref, acc_ref):
    nh, t, dh = q_ref.shape[1], q_ref.shape[2], q_ref.shape[3]
    past = lfct_ref.shape[2]
    pad = lfnt_ref.shape[2]
    pc = kc_ref.shape[2] // nh
    step = pl.program_id(1)

    @pl.when(step == 0)
    def _():
        triu = _tri(CUMSUM_CHUNK, upper=True)
        total = jnp.zeros((nh, 1), F32)
        for i in range(past // CUMSUM_CHUNK):
            cs = slice(i * CUMSUM_CHUNK, (i + 1) * CUMSUM_CHUNK)
            blk = _dot_ones_rhs(lfct_ref[0, :, cs], triu) + total
            cct_ref[:, cs] = blk
            total = blk[:, CUMSUM_CHUNK - 1:CUMSUM_CHUNK]
        eye = (lax.broadcasted_iota(jnp.int32, (nh, nh), 0)
               == lax.broadcasted_iota(jnp.int32, (nh, nh), 1))
        total_row = jnp.sum(jnp.where(eye, total, 0.0), axis=0, keepdims=True)
        cn_ref[...] = _dot_ones_lhs(_tri(t, upper=False), lfn_ref[0]) + total_row
        cnt_ref[...] = _dot_ones_rhs(lfnt_ref[0], _tri(pad, upper=True)) + total
        m_ref[...] = jnp.full_like(m_ref, -jnp.inf)
        l_ref[...] = jnp.zeros_like(l_ref)
        acc_ref[...] = jnp.zeros_like(acc_ref)

    def update(hd, s, v):
        m = m_ref[hd]
        m_new = jnp.maximum(m, jnp.max(s, axis=1, keepdims=True))
        alpha = jnp.exp(m - m_new)
        p = jnp.exp(s - m_new)
        m_ref[hd] = m_new
        l_ref[hd] = alpha * l_ref[hd] + jnp.sum(p, axis=1, keepdims=True)
        acc_ref[hd] = alpha * acc_ref[hd] + _dot(p.astype(BF16), v)

    p0 = pl.multiple_of(step * pc, pc)
    for hd in range(nh):
        kc = kc_ref.at[0, 0][pl.ds(hd, pc, stride=nh), :].astype(BF16)
        vc = vc_ref.at[0, 0][pl.ds(hd, pc, stride=nh), :].astype(BF16)
        s = (lax.dot_general(q_ref[0, hd], kc, _NT, preferred_element_type=F32)
             + cn_ref[:, hd:hd + 1] - cct_ref[hd:hd + 1, pl.ds(p0, pc)])
        update(hd, s, vc)

    @pl.when(step == pl.num_programs(1) - 1)
    def _():
        row = lax.broadcasted_iota(jnp.int32, (t, pad), 0)
        col = lax.broadcasted_iota(jnp.int32, (t, pad), 1)
        zeros = jnp.zeros((pad - t, dh), BF16)
        for hd in range(nh):
            kn = jnp.concatenate([kn_ref[0, hd], zeros], axis=0)
            vn = jnp.concatenate([vn_ref[0, hd], zeros], axis=0)
            s = (lax.dot_general(q_ref[0, hd], kn, _NT, preferred_element_type=F32)
                 + cn_ref[:, hd:hd + 1] - cnt_ref[hd:hd + 1, :])
            update(hd, jnp.where(col <= row, s, -jnp.inf), vn)
            o_ref[:, hd * dh:(hd + 1) * dh] = (acc_ref[hd] / l_ref[hd]).astype(BF16)


def _fox_sample_attn(q, kn, vn, k_cache, v_cache, layer, logf_cache_t, logf_new, logf_new_t, *, batch):
    _, nh, n, dh = q.shape
    t = n // batch
    past = k_cache.shape[2] // nh
    pc = CACHE_CHUNK
    assert past % pc == 0 and past % CUMSUM_CHUNK == 0 and dh == LANES
    new_spec = pl.BlockSpec((1, nh, t, dh), lambda bi, si: (0, 0, bi, 0))
    cache_spec = pl.BlockSpec((1, 1, pc * nh, dh), lambda bi, si: (layer, bi, si, 0))
    whole = lambda shape: pl.BlockSpec((1,) + shape[1:], lambda bi, si: (bi, 0, 0))
    pad = logf_new_t.shape[2]
    return pl.pallas_call(
        _fox_sample_attn_kernel,
        grid=(batch, past // pc),
        in_specs=[new_spec, cache_spec, cache_spec, new_spec, new_spec,
                  whole(logf_cache_t.shape), whole(logf_new.shape), whole(logf_new_t.shape)],
        out_specs=pl.BlockSpec((t, nh * dh), lambda bi, si: (bi, 0)),
        out_shape=jax.ShapeDtypeStruct((n, nh * dh), BF16),
        scratch_shapes=[pltpu.VMEM((nh, past), F32), pltpu.VMEM((nh, pad), F32),
                        pltpu.VMEM((t, nh), F32), pltpu.VMEM((nh, t, 1), F32),
                        pltpu.VMEM((nh, t, 1), F32), pltpu.VMEM((nh, t, dh), F32)],
        compiler_params=_params(2),
        name="fox_sample_attn",
    )(q, k_cache, v_cache, kn, vn, logf_cache_t, logf_new, logf_new_t)


def _fox_post_kernel(o_ref, sz_ref, x_ref, p_ref, gpost_ref, wout_ref, wgate_ref, wproj_ref, out_ref):
    o = _dot(o_ref[...] * sz_ref[...], wout_ref[...])
    out_ref[...] = _residual_and_embedding(x_ref[...], o, p_ref[...], gpost_ref, wgate_ref, wproj_ref)


def _fox_post(o, sz, x, p, gpost, w_out, w_gate, w_proj, *, tokens):
    n, d = x.shape
    e = o.shape[1]
    assert n % tokens == 0
    row = lambda i: (i, 0)
    return pl.pallas_call(
        _fox_post_kernel,
        grid=(n // tokens,),
        in_specs=[pl.BlockSpec((tokens, e), row), pl.BlockSpec((tokens, e), row),
                  pl.BlockSpec((tokens, d), row), pl.BlockSpec((tokens, p.shape[1]), row),
                  _resident(gpost.shape), _resident(w_out.shape), _resident(w_gate.shape),
                  _resident(w_proj.shape)],
        out_specs=pl.BlockSpec((tokens, d), row),
        out_shape=jax.ShapeDtypeStruct((n, d), F32),
        compiler_params=_params(1),
        name="fox_post",
    )(o, sz, x, p, gpost, w_out, w_gate, w_proj)


def _mix_operands(w_s, b_s, rows, gd):
    n = min(rows, GMLP_BLOCK)
    reps = GMLP_BLOCK // n
    w = jnp.tile(w_s[:, :n, :n], (1, reps, reps))
    b = jnp.repeat(jnp.tile(b_s[:, :n], (1, reps)).T, gd, axis=1)
    return w, b, n


def kernel(x_prompt, x_sample, cache_fox_k, cache_fox_v, cache_fox_logf, p_prompt, p_sample, norm_pre, norm_post, gmlp_w_in, gmlp_ln_g, gmlp_ln_b, gmlp_w_s, gmlp_b_s, gmlp_w_out, fox_w_in, fox_b_f, fox_w_out, ple_w_proj, ple_w_gate):
    bp, sp, d = x_prompt.shape
    bs, ts, _ = x_sample.shape
    depth = norm_pre.shape[0]
    n_fox, _, past, nh, dh = cache_fox_k.shape
    e = nh * dh
    groups = gmlp_w_s.shape[1]
    gd = e // groups
    n_p, n_s = bp * sp, bs * ts

    xp = x_prompt.reshape(n_p, d)
    xs = x_sample.reshape(n_s, d)
    cache_k = cache_fox_k.reshape(cache_fox_k.shape[0], bs, past * nh, dh)
    cache_v = cache_fox_v.reshape(cache_fox_v.shape[0], bs, past * nh, dh)
    gmlp_v_s = []
    flf_p, flf_s = [], []
    kv_p = (jnp.zeros((n_fox, bp, sp * nh, dh), F32),) * 2
    kv_s = (jnp.zeros((n_fox, 1, n_s * nh, dh), F32),) * 2
    for i in range(depth):
        j = i // 2
        gpre, gpost = norm_pre[i][None], norm_post[i][None]
        w_gate, w_proj = ple_w_gate[i].astype(BF16), ple_w_proj[i].astype(BF16)
        pp = p_prompt[i].reshape(n_p, -1)
        ps = p_sample[i].reshape(n_s, -1)
        if i % 2 == 0:
            w_in, w_out = gmlp_w_in[j].astype(BF16), gmlp_w_out[j].astype(BF16)
            ln_g, ln_b = gmlp_ln_g[j][None], gmlp_ln_b[j][None]
            wm_p, bm_p, _ = _mix_operands(gmlp_w_s[j], gmlp_b_s[j], sp, gd)
            wm_s, bm_s, n_mix = _mix_operands(gmlp_w_s[j], gmlp_b_s[j], ts, gd)
            xp, _ = _gmlp_layer(xp, pp, gpre, gpost, w_in, ln_g, ln_b, wm_p, bm_p, w_out, w_gate, w_proj,
                                tokens=GMLP_TOKENS, span=CHUNK, causal=True, emit_v=False)
            assert n_mix <= CHUNK and n_s % GMLP_BLOCK == 0
            xs, v_s = _gmlp_layer(xs, ps, gpre, gpost, w_in, ln_g, ln_b, wm_s, bm_s, w_out, w_gate, w_proj,
                                  tokens=GMLP_BLOCK, span=n_mix, causal=False, emit_v=True)
            gmlp_v_s.append(v_s.reshape(bs, ts, e))
        else:
            w = fox_w_in[j][:, :4 * e].astype(BF16)
            wf = fox_w_in[j][:, 4 * e:].astype(BF16)
            bf = fox_b_f[j][None]
            w_out = fox_w_out[j].astype(BF16)

            q, kb, vt, *kv_p, sz, lf = _fox_proj(xp, gpre, w, wf, bf, kv_p, layer=j, groups=bp, rows=sp,
                                                 heads=nh, tokens=PROJ_TOKENS, folded_bias=True)
            o = _fox_attn(q, kb, vt)
            xp = _fox_post(o.reshape(n_p, e), sz, xp, pp, gpost, w_out, w_gate, w_proj, tokens=POST_TOKENS)
            flf_p.append(lf.reshape(bp, sp, nh))

            q, kb, vb, *kv_s, sz, lf = _fox_proj(xs, gpre, w, wf, bf, kv_s, layer=j, groups=1, rows=n_s,
                                                 heads=nh, tokens=n_s, folded_bias=False)
            lf3 = lf.reshape(bs, ts, nh)
            lf3_t = jnp.pad(lf3.transpose(0, 2, 1), ((0, 0), (0, 0), (0, LANES - ts)))
            o = _fox_sample_attn(q, kb, vb, cache_k, cache_v, j, cache_fox_logf[j].transpose(0, 2, 1),
                                 lf3, lf3_t, batch=bs)
            xs = _fox_post(o, sz, xs, ps, gpost, w_out, w_gate, w_proj, tokens=n_s)
            flf_s.append(lf3)
    fk_p, fv_p = (a.reshape(n_fox, bp, sp, nh, dh) for a in kv_p)
    fk_s, fv_s = (a.reshape(n_fox, bs, ts, nh, dh) for a in kv_s)
    return (xp.reshape(bp, sp, d), xs.reshape(bs, ts, d), jnp.stack(gmlp_v_s),
            fk_p, fv_p, jnp.stack(flf_p), fk_s, fv_s, jnp.stack(flf_s))
```

```python
import functools

import jax
import jax.numpy as jnp
from jax import lax
from jax.experimental import pallas as pl
from jax.experimental.pallas import tpu as pltpu

F32 = jnp.float32
BF16 = jnp.bfloat16

CHUNK = 64
GMLP_BLOCK = 128
RMS_EPS = 1e-6
LN_EPS = 1e-5
LANES = 128
VMEM_LIMIT_BYTES = 56 * 1024 * 1024

GMLP_TOKENS = 256
PROJ_TOKENS = 256
POST_TOKENS = 512
ATTN_QUERIES = 512
ATTN_KEYS = 512
ATTN_HEADS = 4
CACHE_CHUNK = 256
LOG2E = 1.4426950408889634
CUMSUM_CHUNK = 256

_NT = (((1,), (1,)), ((), ()))


def _params(n_axes):
    return pltpu.CompilerParams(
        dimension_semantics=("arbitrary",) * n_axes, vmem_limit_bytes=VMEM_LIMIT_BYTES)


def _resident(shape):
    zeros = (0,) * len(shape)
    return pl.BlockSpec(shape, lambda *_: zeros, pipeline_mode=pl.Buffered(1))


def _dot(a, b):
    return jnp.dot(a, b, preferred_element_type=F32)


def _rms_norm(x, g):
    return x * lax.rsqrt(jnp.mean(x * x, axis=-1, keepdims=True) + RMS_EPS) * g


def _silu(z):
    return z * jax.nn.sigmoid(z)


def _split3(x):
    a = x.astype(BF16)
    r = x - a.astype(F32)
    b = r.astype(BF16)
    c = (r - b.astype(F32)).astype(BF16)
    return a, b, c


def _dot_ones_lhs(ones, x):
    a, b, c = _split3(x)
    return (_dot(ones, a) + _dot(ones, b)) + _dot(ones, c)


def _dot_ones_rhs(x, ones):
    a, b, c = _split3(x)
    return (_dot(a, ones) + _dot(b, ones)) + _dot(c, ones)


def _tri(n, upper):
    r = lax.broadcasted_iota(jnp.int32, (n, n), 0)
    c = lax.broadcasted_iota(jnp.int32, (n, n), 1)
    keep = (r <= c) if upper else (c <= r)
    return jnp.where(keep, 1.0, 0.0).astype(BF16)


def _residual_and_embedding(x, o, p, gpost_ref, wgate_ref, wproj_ref):
    x1 = x + _rms_norm(o, gpost_ref[...])
    gate = jax.nn.sigmoid(_dot(x1.astype(BF16), wgate_ref[...]))
    return x1 + gate * _dot(p.astype(BF16), wproj_ref[...])


def _gmlp_layer_kernel(x_ref, p_ref, gpre_ref, gpost_ref, win_ref, lng_ref, lnb_ref,
                       wmix_ref, bmix_ref, wout_ref, wgate_ref, wproj_ref, *rest,
                       span_shift, causal, emit_v):
    if emit_v:
        out_ref, v_ref, y_ref = rest
    else:
        out_ref, y_ref = rest
    tm = x_ref.shape[0]
    e = wout_ref.shape[0]
    groups, blk, _ = wmix_ref.shape
    gd = e // groups

    x = x_ref[...]
    h = _rms_norm(x, gpre_ref[...]).astype(BF16)
    u = jax.nn.gelu(_dot(h, win_ref[:, 0:e]))
    gv = jax.nn.gelu(_dot(h, win_ref[:, e:2 * e]))
    sz = _silu(_dot(h, win_ref[:, 2 * e:3 * e]))
    d = gv - jnp.mean(gv, axis=-1, keepdims=True)
    v = d * lax.rsqrt(jnp.mean(d * d, axis=-1, keepdims=True) + LN_EPS) * lng_ref[...] + lnb_ref[...]
    if emit_v:
        v_ref[...] = v
    vb = v.astype(BF16)

    qs = lax.shift_right_logical(lax.broadcasted_iota(jnp.int32, (blk, blk), 0), span_shift)
    ks = lax.shift_right_logical(lax.broadcasted_iota(jnp.int32, (blk, blk), 1), span_shift)
    allowed = (ks <= qs) if causal else (ks == qs)
    for g in range(groups):
        cs = slice(g * gd, (g + 1) * gd)
        wm = jnp.where(allowed, wmix_ref[g], 0.0).astype(BF16)
        for r in range(tm // blk):
            rs = slice(r * blk, (r + 1) * blk)
            s = _dot(wm, vb[rs, cs]) + bmix_ref[:, cs]
            y_ref[rs, cs] = (u[rs, cs] * s * sz[rs, cs]).astype(BF16)
    o = _dot(y_ref[...], wout_ref[...])
    out_ref[...] = _residual_and_embedding(x, o, p_ref[...], gpost_ref, wgate_ref, wproj_ref)


def _gmlp_layer(x, p, gpre, gpost, w_in, ln_g, ln_b, w_mix, b_mix, w_out, w_gate, w_proj,
                *, tokens, span, causal, emit_v):
    n, d = x.shape
    e = w_out.shape[0]
    pd = p.shape[1]
    assert n % tokens == 0 and tokens % GMLP_BLOCK == 0 and span & (span - 1) == 0
    row = lambda i: (i, 0)
    out_shape = [jax.ShapeDtypeStruct((n, d), F32)]
    out_specs = [pl.BlockSpec((tokens, d), row)]
    if emit_v:
        out_shape.append(jax.ShapeDtypeStruct((n, e), F32))
        out_specs.append(pl.BlockSpec((tokens, e), row))
    res = pl.pallas_call(
        functools.partial(_gmlp_layer_kernel, span_shift=span.bit_length() - 1,
                          causal=causal, emit_v=emit_v),
        grid=(n // tokens,),
        in_specs=[
            pl.BlockSpec((tokens, d), row),
            pl.BlockSpec((tokens, pd), row),
            _resident(gpre.shape), _resident(gpost.shape), _resident(w_in.shape),
            _resident(ln_g.shape), _resident(ln_b.shape), _resident(w_mix.shape),
            _resident(b_mix.shape), _resident(w_out.shape), _resident(w_gate.shape),
            _resident(w_proj.shape),
        ],
        out_specs=out_specs,
        out_shape=out_shape,
        scratch_shapes=[pltpu.VMEM((tokens, e), BF16)],
        compiler_params=_params(1),
        name="gmlp_layer",
    )(x, p, gpre, gpost, w_in, ln_g, ln_b, w_mix, b_mix, w_out, w_gate, w_proj)
    return res if emit_v else (res[0], None)


def _bias_columns(c, hd, width):
    hi, mid, lo = _split3(c[:, hd:hd + 1])
    shape = (c.shape[0], width)
    return tuple(jnp.broadcast_to(t.astype(F32), shape) for t in (hi, mid, lo))


def _fox_proj_kernel(x_ref, gpre_ref, w_ref, wf_ref, bf_ref, kf_in_ref, vf_in_ref,
                     q_ref, k_ref, v_ref, kf_ref, vf_ref, sz_ref, logf_ref, *scratch, scale, folded_bias):
    del kf_in_ref, vf_in_ref
    tm, e = sz_ref.shape
    nh = q_ref.shape[1]
    dh = e // nh
    h = _rms_norm(x_ref[...], gpre_ref[...]).astype(BF16)
    logf = jax.nn.log_sigmoid(_dot(h, wf_ref[...]) + bf_ref[...])
    logf_ref[...] = logf

    q = _dot(h, w_ref[:, 0:e]) * scale
    k = _dot(h, w_ref[:, e:2 * e])
    v = _dot(h, w_ref[:, 2 * e:3 * e])
    for hd in range(nh):
        hs = slice(hd * dh, (hd + 1) * dh)
        kf_ref.at[0, 0][pl.ds(hd, tm, stride=nh), :] = k[:, hs]
        vf_ref.at[0, 0][pl.ds(hd, tm, stride=nh), :] = v[:, hs]
    sz_ref[...] = _silu(_dot(h, w_ref[:, 3 * e:4 * e])).astype(BF16)

    if not folded_bias:
        for hd in range(nh):
            hs = slice(hd * dh, (hd + 1) * dh)
            q_ref[0, hd] = q[:, hs].astype(BF16)
            k_ref[0, hd] = k[:, hs].astype(BF16)
            v_ref[0, hd] = v[:, hs].astype(BF16)
        return

    (carry_ref,) = scratch

    @pl.when(pl.program_id(1) == 0)
    def _():
        carry_ref[...] = jnp.zeros_like(carry_ref)

    c = _dot_ones_lhs(_tri(tm, upper=False), logf) + carry_ref[...]
    carry_ref[...] = c[tm - 1:tm, :]
    c = c * LOG2E
    lane = lax.broadcasted_iota(jnp.int32, (tm, dh), 1)
    ones_q = jnp.where((lane >= 3) & (lane < 6), 1.0, 0.0)
    ones_k = jnp.where(lane < 3, 1.0, 0.0)
    for hd in range(nh):
        hs = slice(hd * dh, (hd + 1) * dh)
        hi, mid, lo = _bias_columns(c, hd, dh)
        aux_q = jnp.where(lane == 0, hi, jnp.where(lane == 1, mid, jnp.where(lane == 2, lo, ones_q)))
        aux_k = jnp.where(lane == 3, -hi, jnp.where(lane == 4, -mid, jnp.where(lane == 5, -lo, ones_k)))
        q_ref[0, hd, :, 0:dh] = q[:, hs].astype(BF16)
        q_ref[0, hd, :, dh:2 * dh] = aux_q.astype(BF16)
        k_ref[0, hd, :, 0:dh] = k[:, hs].astype(BF16)
        k_ref[0, hd, :, dh:2 * dh] = aux_k.astype(BF16)
        v_ref[0, hd] = v[:, hs].T.astype(BF16)


def _fox_proj(x, gpre, w, wf, bf, kv_leaves, *, layer, groups, rows, heads, tokens, folded_bias):
    n, d = x.shape
    e = (w.shape[1]) // 4
    dh = e // heads
    assert n == groups * rows and rows % tokens == 0
    steps = rows // tokens
    tok = lambda b, j: (b * steps + j, 0)
    if folded_bias:
        qk_shape = jax.ShapeDtypeStruct((groups, heads, rows, 2 * dh), BF16)
        qk_spec = pl.BlockSpec((1, heads, tokens, 2 * dh), lambda b, j: (b, 0, j, 0))
        v_shape = jax.ShapeDtypeStruct((groups, heads, dh, rows), BF16)
        v_spec = pl.BlockSpec((1, heads, dh, tokens), lambda b, j: (b, 0, 0, j))
        scratch = [pltpu.VMEM((1, heads), F32)]
        scale = float(dh) ** -0.5 * LOG2E
    else:
        qk_shape = v_shape = jax.ShapeDtypeStruct((groups, heads, rows, dh), BF16)
        qk_spec = v_spec = pl.BlockSpec((1, heads, tokens, dh), lambda b, j: (b, 0, j, 0))
        scratch = []
        scale = float(dh) ** -0.5
    k_leaf, v_leaf = kv_leaves
    assert k_leaf.shape == v_leaf.shape == (k_leaf.shape[0], groups, rows * heads, dh)
    leaf_shape = jax.ShapeDtypeStruct(k_leaf.shape, F32)
    leaf_spec = pl.BlockSpec((1, 1, tokens * heads, dh), lambda b, j: (layer, b, j, 0))
    return pl.pallas_call(
        functools.partial(_fox_proj_kernel, scale=scale, folded_bias=folded_bias),
        grid=(groups, steps),
        in_specs=[pl.BlockSpec((tokens, d), tok), _resident(gpre.shape), _resident(w.shape),
                  _resident(wf.shape), _resident(bf.shape),
                  pl.BlockSpec(memory_space=pl.ANY), pl.BlockSpec(memory_space=pl.ANY)],
        out_specs=[qk_spec, qk_spec, v_spec, leaf_spec, leaf_spec,
                   pl.BlockSpec((tokens, e), tok), pl.BlockSpec((tokens, heads), tok)],
        out_shape=[qk_shape, qk_shape, v_shape, leaf_shape, leaf_shape,
                   jax.ShapeDtypeStruct((n, e), BF16), jax.ShapeDtypeStruct((n, heads), F32)],
        input_output_aliases={5: 3, 6: 4},
        scratch_shapes=scratch,
        compiler_params=_params(2),
        name="fox_proj",
    )(x, gpre, w, wf, bf, k_leaf, v_leaf)


def _fox_attn_kernel(q_ref, k_ref, vt_ref, o_ref, s_ref, smax_ref, m_ref, l_ref, acc_ref):
    hpb, tq = q_ref.shape[1], q_ref.shape[2]
    tk = s_ref.shape[1]
    qi = pl.program_id(2)

    def produce(j):
        k0 = pl.multiple_of(j * tk, tk)
        for hh in range(hpb):
            s = lax.dot_general(k_ref[0, hh, pl.ds(k0, tk), :], q_ref[0, hh], _NT,
                                preferred_element_type=F32)
            s_ref[hh] = s
            smax_ref[hh] = jnp.max(s, axis=0, keepdims=True)

    def consume(j, diagonal):
        k0 = pl.multiple_of(j * tk, tk)
        if diagonal:
            key = lax.broadcasted_iota(jnp.int32, (tk, tq), 0) + k0
            query = lax.broadcasted_iota(jnp.int32, (tk, tq), 1) + qi * tq
            visible = key <= query
        for hh in range(hpb):
            s = s_ref[hh]
            if diagonal:
                s = jnp.where(visible, s, -jnp.inf)
                s_max = jnp.max(s, axis=0, keepdims=True)
            else:
                s_max = smax_ref[hh]
            m = m_ref[hh]
            m_new = jnp.maximum(m, s_max)
            alpha = jnp.exp2(m - m_new)
            p = jnp.exp2(s - m_new)
            m_ref[hh] = m_new
            l_ref[hh] = alpha * l_ref[hh] + jnp.sum(p, axis=0, keepdims=True)
            acc_ref[hh] = alpha * acc_ref[hh] + _dot(vt_ref[0, hh, :, pl.ds(k0, tk)], p.astype(BF16))

    m_ref[...] = jnp.full_like(m_ref, -jnp.inf)
    l_ref[...] = jnp.zeros_like(l_ref)
    acc_ref[...] = jnp.zeros_like(acc_ref)
    n_full = (qi * tq) // tk
    produce(0)

    @pl.loop(0, n_full // 2)
    def _(i):
        consume(2 * i, False)
        produce(2 * i + 1)
        consume(2 * i + 1, False)
        produce(2 * i + 2)

    @pl.when(n_full % 2 == 1)
    def _():
        consume(n_full - 1, False)
        produce(n_full)

    consume(n_full, True)
    dh = acc_ref.shape[1]
    for hh in range(hpb):
        o_ref[0, :, hh * dh:(hh + 1) * dh] = (acc_ref[hh] / l_ref[hh]).T.astype(BF16)


def _fox_attn(q, k, vt):
    b, nh, s, dh2 = q.shape
    dh = vt.shape[2]
    t, tk, hpb = ATTN_QUERIES, ATTN_KEYS, ATTN_HEADS
    assert s % tk == 0 and tk % t == 0 and nh % hpb == 0 and dh == LANES and dh2 == 2 * dh
    return pl.pallas_call(
        _fox_attn_kernel,
        grid=(b, nh // hpb, s // t),
        in_specs=[
            pl.BlockSpec((1, hpb, t, dh2), lambda bi, hi, qi: (bi, hi, qi, 0)),
            pl.BlockSpec((1, hpb, s, dh2), lambda bi, hi, qi: (bi, hi, 0, 0)),
            pl.BlockSpec((1, hpb, dh, s), lambda bi, hi, qi: (bi, hi, 0, 0)),
        ],
        out_specs=pl.BlockSpec((1, t, hpb * dh), lambda bi, hi, qi: (bi, qi, hi)),
        out_shape=jax.ShapeDtypeStruct((b, s, nh * dh), BF16),
        scratch_shapes=[pltpu.VMEM((hpb, tk, t), F32), pltpu.VMEM((hpb, 1, t), F32),
                        pltpu.VMEM((hpb, 1, t), F32), pltpu.VMEM((hpb, 1, t), F32),
                        pltpu.VMEM((hpb, dh, t), F32)],
        compiler_params=_params(3),
        name="fox_attn",
    )(q, k, vt)


def _fox_sample_attn_kernel(q_ref, kc_ref, vc_ref, kn_ref, vn_ref, lfct_ref, lfn_ref, lfnt_ref,
                            o_ref, cct_ref, cnt_ref, cn_ref, m_ref, l_ref, acc_ref, s_ref, sn_ref, pv_ref):
    nh, t, dh = q_ref.shape[1], q_ref.shape[2], q_ref.shape[3]
    past = lfct_ref.shape[2]
    pad = lfnt_ref.shape[2]
    pc = kc_ref.shape[2] // nh
    step = pl.program_id(1)

    @pl.when(step == 0)
    def _():
        triu = _tri(CUMSUM_CHUNK, upper=True)
        total = jnp.zeros((nh, 1), F32)
        for i in range(past // CUMSUM_CHUNK):
            cs = slice(i * CUMSUM_CHUNK, (i + 1) * CUMSUM_CHUNK)
            blk = _dot_ones_rhs(lfct_ref[0, :, cs], triu) + total
            cct_ref[:, cs] = blk
            total = blk[:, CUMSUM_CHUNK - 1:CUMSUM_CHUNK]
        eye = (lax.broadcasted_iota(jnp.int32, (nh, nh), 0)
               == lax.broadcasted_iota(jnp.int32, (nh, nh), 1))
        total_row = jnp.sum(jnp.where(eye, total, 0.0), axis=0, keepdims=True)
        cn_ref[...] = _dot_ones_lhs(_tri(t, upper=False), lfn_ref[0]) + total_row
        cnt_ref[...] = _dot_ones_rhs(lfnt_ref[0], _tri(pad, upper=True)) + total
        m_ref[...] = jnp.full_like(m_ref, -jnp.inf)
        l_ref[...] = jnp.zeros_like(l_ref)
        acc_ref[...] = jnp.zeros_like(acc_ref)

    def rows(hd):
        return slice(hd * t, (hd + 1) * t)

    def attend(s, values):
        m = m_ref[...]
        m_new = jnp.maximum(m, jnp.max(s, axis=1, keepdims=True))
        alpha = jnp.exp(m - m_new)
        p = jnp.exp(s - m_new)
        m_ref[...] = m_new
        l_ref[...] = alpha * l_ref[...] + jnp.sum(p, axis=1, keepdims=True)
        p = p.astype(BF16)
        for hd in range(nh):
            pv_ref[rows(hd), :] = _dot(p[rows(hd), :], values(hd))
        acc_ref[...] = alpha * acc_ref[...] + pv_ref[...]

    p0 = pl.multiple_of(step * pc, pc)
    for hd in range(nh):
        kc = kc_ref.at[0, 0][pl.ds(hd, pc, stride=nh), :].astype(BF16)
        s_ref[rows(hd), :] = (lax.dot_general(q_ref[0, hd], kc, _NT, preferred_element_type=F32)
                              + cn_ref[:, hd:hd + 1] - cct_ref[hd:hd + 1, pl.ds(p0, pc)])
    attend(s_ref[...], lambda hd: vc_ref.at[0, 0][pl.ds(hd, pc, stride=nh), :].astype(BF16))

    @pl.when(step == pl.num_programs(1) - 1)
    def _():
        visible = (lax.broadcasted_iota(jnp.int32, (t, pad), 1)
                   <= lax.broadcasted_iota(jnp.int32, (t, pad), 0))
        zeros = jnp.zeros((pad - t, dh), BF16)
        for hd in range(nh):
            kn = jnp.concatenate([kn_ref[0, hd], zeros], axis=0)
            s = (lax.dot_general(q_ref[0, hd], kn, _NT, preferred_element_type=F32)
                 + cn_ref[:, hd:hd + 1] - cnt_ref[hd:hd + 1, :])
            sn_ref[rows(hd), :] = jnp.where(visible, s, -jnp.inf)
        attend(sn_ref[...], lambda hd: jnp.concatenate([vn_ref[0, hd], zeros], axis=0))
        o = (acc_ref[...] / l_ref[...]).astype(BF16)
        for hd in range(nh):
            o_ref[:, hd * dh:(hd + 1) * dh] = o[rows(hd), :]


def _fox_sample_attn(q, kn, vn, k_cache, v_cache, layer, logf_cache_t, logf_new, logf_new_t, *, batch):
    _, nh, n, dh = q.shape
    t = n // batch
    past = k_cache.shape[2] // nh
    pc = CACHE_CHUNK
    assert past % pc == 0 and past % CUMSUM_CHUNK == 0 and dh == LANES
    new_spec = pl.BlockSpec((1, nh, t, dh), lambda bi, si: (0, 0, bi, 0))
    cache_spec = pl.BlockSpec((1, 1, pc * nh, dh), lambda bi, si: (layer, bi, si, 0))
    whole = lambda shape: pl.BlockSpec((1,) + shape[1:], lambda bi, si: (bi, 0, 0))
    pad = logf_new_t.shape[2]
    return pl.pallas_call(
        _fox_sample_attn_kernel,
        grid=(batch, past // pc),
        in_specs=[new_spec, cache_spec, cache_spec, new_spec, new_spec,
                  whole(logf_cache_t.shape), whole(logf_new.shape), whole(logf_new_t.shape)],
        out_specs=pl.BlockSpec((t, nh * dh), lambda bi, si: (bi, 0)),
        out_shape=jax.ShapeDtypeStruct((n, nh * dh), BF16),
        scratch_shapes=[pltpu.VMEM((nh, past), F32), pltpu.VMEM((nh, pad), F32),
                        pltpu.VMEM((t, nh), F32), pltpu.VMEM((nh * t, 1), F32),
                        pltpu.VMEM((nh * t, 1), F32), pltpu.VMEM((nh * t, dh), F32),
                        pltpu.VMEM((nh * t, pc), F32), pltpu.VMEM((nh * t, pad), F32),
                        pltpu.VMEM((nh * t, dh), F32)],
        compiler_params=_params(2),
        name="fox_sample_attn",
    )(q, k_cache, v_cache, kn, vn, logf_cache_t, logf_new, logf_new_t)


def _fox_post_kernel(o_ref, sz_ref, x_ref, p_ref, gpost_ref, wout_ref, wgate_ref, wproj_ref, out_ref):
    o = _dot(o_ref[...] * sz_ref[...], wout_ref[...])
    out_ref[...] = _residual_and_embedding(x_ref[...], o, p_ref[...], gpost_ref, wgate_ref, wproj_ref)


def _fox_post(o, sz, x, p, gpost, w_out, w_gate, w_proj, *, tokens):
    n, d = x.shape
    e = o.shape[1]
    assert n % tokens == 0
    row = lambda i: (i, 0)
    return pl.pallas_call(
        _fox_post_kernel,
        grid=(n // tokens,),
        in_specs=[pl.BlockSpec((tokens, e), row), pl.BlockSpec((tokens, e), row),
                  pl.BlockSpec((tokens, d), row), pl.BlockSpec((tokens, p.shape[1]), row),
                  _resident(gpost.shape), _resident(w_out.shape), _resident(w_gate.shape),
                  _resident(w_proj.shape)],
        out_specs=pl.BlockSpec((tokens, d), row),
        out_shape=jax.ShapeDtypeStruct((n, d), F32),
        compiler_params=_params(1),
        name="fox_post",
    )(o, sz, x, p, gpost, w_out, w_gate, w_proj)


def _mix_operands(w_s, b_s, rows, gd):
    n = min(rows, GMLP_BLOCK)
    reps = GMLP_BLOCK // n
    w = jnp.tile(w_s[:, :n, :n], (1, reps, reps))
    b = jnp.repeat(jnp.tile(b_s[:, :n], (1, reps)).T, gd, axis=1)
    return w, b, n


def kernel(x_prompt, x_sample, cache_fox_k, cache_fox_v, cache_fox_logf, p_prompt, p_sample, norm_pre, norm_post, gmlp_w_in, gmlp_ln_g, gmlp_ln_b, gmlp_w_s, gmlp_b_s, gmlp_w_out, fox_w_in, fox_b_f, fox_w_out, ple_w_proj, ple_w_gate):
    bp, sp, d = x_prompt.shape
    bs, ts, _ = x_sample.shape
    depth = norm_pre.shape[0]
    n_fox, _, past, nh, dh = cache_fox_k.shape
    e = nh * dh
    groups = gmlp_w_s.shape[1]
    gd = e // groups
    n_p, n_s = bp * sp, bs * ts

    xp = x_prompt.reshape(n_p, d)
    xs = x_sample.reshape(n_s, d)
    cache_k = cache_fox_k.reshape(cache_fox_k.shape[0], bs, past * nh, dh)
    cache_v = cache_fox_v.reshape(cache_fox_v.shape[0], bs, past * nh, dh)
    gmlp_v_s = []
    flf_p, flf_s = [], []
    kv_p = (jnp.zeros((n_fox, bp, sp * nh, dh), F32),) * 2
    kv_s = (jnp.zeros((n_fox, 1, n_s * nh, dh), F32),) * 2
    for i in range(depth):
        j = i // 2
        gpre, gpost = norm_pre[i][None], norm_post[i][None]
        w_gate, w_proj = ple_w_gate[i].astype(BF16), ple_w_proj[i].astype(BF16)
        pp = p_prompt[i].reshape(n_p, -1)
        ps = p_sample[i].reshape(n_s, -1)
        if i % 2 == 0:
            w_in, w_out = gmlp_w_in[j].astype(BF16), gmlp_w_out[j].astype(BF16)
            ln_g, ln_b = gmlp_ln_g[j][None], gmlp_ln_b[j][None]
            wm_p, bm_p, _ = _mix_operands(gmlp_w_s[j], gmlp_b_s[j], sp, gd)
            wm_s, bm_s, n_mix = _mix_operands(gmlp_w_s[j], gmlp_b_s[j], ts, gd)
            xp, _ = _gmlp_layer(xp, pp, gpre, gpost, w_in, ln_g, ln_b, wm_p, bm_p, w_out, w_gate, w_proj,
                                tokens=GMLP_TOKENS, span=CHUNK, causal=True, emit_v=False)
            assert n_mix <= CHUNK and n_s % GMLP_BLOCK == 0
            xs, v_s = _gmlp_layer(xs, ps, gpre, gpost, w_in, ln_g, ln_b, wm_s, bm_s, w_out, w_gate, w_proj,
                                  tokens=GMLP_BLOCK, span=n_mix, causal=False, emit_v=True)
            gmlp_v_s.append(v_s.reshape(bs, ts, e))
        else:
            w = fox_w_in[j][:, :4 * e].astype(BF16)
            wf = fox_w_in[j][:, 4 * e:].astype(BF16)
            bf = fox_b_f[j][None]
            w_out = fox_w_out[j].astype(BF16)

            q, kb, vt, *kv_p, sz, lf = _fox_proj(xp, gpre, w, wf, bf, kv_p, layer=j, groups=bp, rows=sp,
                                                 heads=nh, tokens=PROJ_TOKENS, folded_bias=True)
            o = _fox_attn(q, kb, vt)
            xp = _fox_post(o.reshape(n_p, e), sz, xp, pp, gpost, w_out, w_gate, w_proj, tokens=POST_TOKENS)
            flf_p.append(lf.reshape(bp, sp, nh))

            q, kb, vb, *kv_s, sz, lf = _fox_proj(xs, gpre, w, wf, bf, kv_s, layer=j, groups=1, rows=n_s,
                                                 heads=nh, tokens=n_s, folded_bias=False)
            lf3 = lf.reshape(bs, ts, nh)
            lf3_t = jnp.pad(lf3.transpose(0, 2, 1), ((0, 0), (0, 0), (0, LANES - ts)))
            o = _fox_sample_attn(q, kb, vb, cache_k, cache_v, j, cache_fox_logf[j].transpose(0, 2, 1),
                                 lf3, lf3_t, batch=bs)
            xs = _fox_post(o, sz, xs, ps, gpost, w_out, w_gate, w_proj, tokens=n_s)
            flf_s.append(lf3)
    fk_p, fv_p = (a.reshape(n_fox, bp, sp, nh, dh) for a in kv_p)
    fk_s, fv_s = (a.reshape(n_fox, bs, ts, nh, dh) for a in kv_s)
    return (xp.reshape(bp, sp, d), xs.reshape(bs, ts, d), jnp.stack(gmlp_v_s),
            fk_p, fv_p, jnp.stack(flf_p), fk_s, fv_s, jnp.stack(flf_s))
```

```python
import functools

import jax
import jax.numpy as jnp
from jax import lax
from jax.experimental import pallas as pl
from jax.experimental.pallas import tpu as pltpu

F32 = jnp.float32
BF16 = jnp.bfloat16

CHUNK = 64
GMLP_BLOCK = 128
RMS_EPS = 1e-6
LN_EPS = 1e-5
LANES = 128
VMEM_LIMIT_BYTES = 56 * 1024 * 1024

GMLP_TOKENS = 256
PROJ_TOKENS = 256
POST_TOKENS = 512
ATTN_QUERIES = 512
ATTN_KEYS = 512
ATTN_HEADS = 4
CACHE_CHUNK = 256
LOG2E = 1.4426950408889634
CUMSUM_CHUNK = 256

_NT = (((1,), (1,)), ((), ()))


def _params(n_axes):
    return pltpu.CompilerParams(
        dimension_semantics=("arbitrary",) * n_axes, vmem_limit_bytes=VMEM_LIMIT_BYTES)


def _resident(shape):
    zeros = (0,) * len(shape)
    return pl.BlockSpec(shape, lambda *_: zeros, pipeline_mode=pl.Buffered(1))


def _dot(a, b):
    return jnp.dot(a, b, preferred_element_type=F32)


def _rms_norm(x, g):
    return x * lax.rsqrt(jnp.mean(x * x, axis=-1, keepdims=True) + RMS_EPS) * g


def _silu(z):
    return z * jax.nn.sigmoid(z)


def _split3(x):
    a = x.astype(BF16)
    r = x - a.astype(F32)
    b = r.astype(BF16)
    c = (r - b.astype(F32)).astype(BF16)
    return a, b, c


def _dot_ones_lhs(ones, x):
    a, b, c = _split3(x)
    return (_dot(ones, a) + _dot(ones, b)) + _dot(ones, c)


def _dot_ones_rhs(x, ones):
    a, b, c = _split3(x)
    return (_dot(a, ones) + _dot(b, ones)) + _dot(c, ones)


def _tri(n, upper):
    r = lax.broadcasted_iota(jnp.int32, (n, n), 0)
    c = lax.broadcasted_iota(jnp.int32, (n, n), 1)
    keep = (r <= c) if upper else (c <= r)
    return jnp.where(keep, 1.0, 0.0).astype(BF16)


def _residual_and_embedding(x, o, p, gpost_ref, wgate_ref, wproj_ref):
    x1 = x + _rms_norm(o, gpost_ref[...])
    gate = jax.nn.sigmoid(_dot(x1.astype(BF16), wgate_ref[...]))
    return x1 + gate * _dot(p.astype(BF16), wproj_ref[...])


def _gmlp_layer_kernel(x_ref, p_ref, gpre_ref, gpost_ref, win_ref, lng_ref, lnb_ref,
                       wmix_ref, bmix_ref, wout_ref, wgate_ref, wproj_ref, *rest,
                       span_shift, causal, emit_v):
    if emit_v:
        out_ref, v_ref, y_ref = rest
    else:
        out_ref, y_ref = rest
    tm = x_ref.shape[0]
    e = wout_ref.shape[0]
    groups, blk, _ = wmix_ref.shape
    gd = e // groups

    x = x_ref[...]
    h = _rms_norm(x, gpre_ref[...]).astype(BF16)
    u = jax.nn.gelu(_dot(h, win_ref[:, 0:e]))
    gv = jax.nn.gelu(_dot(h, win_ref[:, e:2 * e]))
    sz = _silu(_dot(h, win_ref[:, 2 * e:3 * e]))
    d = gv - jnp.mean(gv, axis=-1, keepdims=True)
    v = d * lax.rsqrt(jnp.mean(d * d, axis=-1, keepdims=True) + LN_EPS) * lng_ref[...] + lnb_ref[...]
    if emit_v:
        v_ref[...] = v
    vb = v.astype(BF16)

    qs = lax.shift_right_logical(lax.broadcasted_iota(jnp.int32, (blk, blk), 0), span_shift)
    ks = lax.shift_right_logical(lax.broadcasted_iota(jnp.int32, (blk, blk), 1), span_shift)
    allowed = (ks <= qs) if causal else (ks == qs)
    row_blocks = [slice(r * blk, (r + 1) * blk) for r in range(tm // blk)]
    for g in range(groups):
        cs = slice(g * gd, (g + 1) * gd)
        wm = jnp.where(allowed, wmix_ref[g], 0.0).astype(BF16)
        mixed = _dot(wm, jnp.concatenate([vb[rs, cs] for rs in row_blocks], axis=1))
        for r, rs in enumerate(row_blocks):
            s = mixed[:, r * gd:(r + 1) * gd] + bmix_ref[:, cs]
            y_ref[rs, cs] = (u[rs, cs] * s * sz[rs, cs]).astype(BF16)
    o = _dot(y_ref[...], wout_ref[...])
    out_ref[...] = _residual_and_embedding(x, o, p_ref[...], gpost_ref, wgate_ref, wproj_ref)


def _gmlp_layer(x, p, gpre, gpost, w_in, ln_g, ln_b, w_mix, b_mix, w_out, w_gate, w_proj,
                *, tokens, span, causal, emit_v):
    n, d = x.shape
    e = w_out.shape[0]
    pd = p.shape[1]
    assert n % tokens == 0 and tokens % GMLP_BLOCK == 0 and span & (span - 1) == 0
    row = lambda i: (i, 0)
    out_shape = [jax.ShapeDtypeStruct((n, d), F32)]
    out_specs = [pl.BlockSpec((tokens, d), row)]
    if emit_v:
        out_shape.append(jax.ShapeDtypeStruct((n, e), F32))
        out_specs.append(pl.BlockSpec((tokens, e), row))
    res = pl.pallas_call(
        functools.partial(_gmlp_layer_kernel, span_shift=span.bit_length() - 1,
                          causal=causal, emit_v=emit_v),
        grid=(n // tokens,),
        in_specs=[
            pl.BlockSpec((tokens, d), row),
            pl.BlockSpec((tokens, pd), row),
            _resident(gpre.shape), _resident(gpost.shape), _resident(w_in.shape),
            _resident(ln_g.shape), _resident(ln_b.shape), _resident(w_mix.shape),
            _resident(b_mix.shape), _resident(w_out.shape), _resident(w_gate.shape),
            _resident(w_proj.shape),
        ],
        out_specs=out_specs,
        out_shape=out_shape,
        scratch_shapes=[pltpu.VMEM((tokens, e), BF16)],
        compiler_params=_params(1),
        name="gmlp_layer",
    )(x, p, gpre, gpost, w_in, ln_g, ln_b, w_mix, b_mix, w_out, w_gate, w_proj)
    return res if emit_v else (res[0], None)


def _bias_columns(c, hd, width):
    hi, mid, lo = _split3(c[:, hd:hd + 1])
    shape = (c.shape[0], width)
    return tuple(jnp.broadcast_to(t.astype(F32), shape) for t in (hi, mid, lo))


def _fox_proj_kernel(x_ref, gpre_ref, w_ref, wf_ref, bf_ref, kf_in_ref, vf_in_ref,
                     q_ref, k_ref, v_ref, kf_ref, vf_ref, sz_ref, logf_ref, *scratch, scale, folded_bias):
    del kf_in_ref, vf_in_ref
    tm, e = sz_ref.shape
    nh = q_ref.shape[1]
    dh = e // nh
    heads = [(hd, slice(hd * dh, (hd + 1) * dh)) for hd in range(nh)]
    h = _rms_norm(x_ref[...], gpre_ref[...]).astype(BF16)
    logf = jax.nn.log_sigmoid(_dot(h, wf_ref[...]) + bf_ref[...])
    logf_ref[...] = logf

    if folded_bias:
        (carry_ref,) = scratch

        @pl.when(pl.program_id(1) == 0)
        def _():
            carry_ref[...] = jnp.zeros_like(carry_ref)

        c = _dot_ones_lhs(_tri(tm, upper=False), logf) + carry_ref[...]
        carry_ref[...] = c[tm - 1:tm, :]
        c = c * LOG2E
        lane = lax.broadcasted_iota(jnp.int32, (tm, dh), 1)
        third = lane - jnp.where(lane >= 3, 3, 0)
        in_q, in_k = lane < 3, (lane >= 3) & (lane < 6)
        ones_q = jnp.where(in_k, 1.0, 0.0)
        ones_k = jnp.where(in_q, 1.0, 0.0)
        for hd, _ in heads:
            hi, mid, lo = _bias_columns(c, hd, dh)
            terms = jnp.where(third == 0, hi, jnp.where(third == 1, mid, lo))
            q_ref[0, hd, dh:2 * dh, :] = jnp.where(in_q, terms, ones_q).T.astype(BF16)
            k_ref[0, hd, :, dh:2 * dh] = jnp.where(in_k, -terms, ones_k).astype(BF16)

    q = _dot(h, w_ref[:, 0:e]) * scale
    for hd, hs in heads:
        if folded_bias:
            q_ref[0, hd, 0:dh, :] = q[:, hs].T.astype(BF16)
        else:
            q_ref[0, hd] = q[:, hs].astype(BF16)
    k = _dot(h, w_ref[:, e:2 * e])
    for hd, hs in heads:
        kf_ref.at[0, 0][pl.ds(hd, tm, stride=nh), :] = k[:, hs]
        if folded_bias:
            k_ref[0, hd, :, 0:dh] = k[:, hs].astype(BF16)
        else:
            k_ref[0, hd] = k[:, hs].astype(BF16)
    v = _dot(h, w_ref[:, 2 * e:3 * e])
    for hd, hs in heads:
        vf_ref.at[0, 0][pl.ds(hd, tm, stride=nh), :] = v[:, hs]
        v_ref[0, hd] = (v[:, hs].T if folded_bias else v[:, hs]).astype(BF16)
    sz_ref[...] = _silu(_dot(h, w_ref[:, 3 * e:4 * e])).astype(BF16)


def _fox_proj(x, gpre, w, wf, bf, kv_leaves, *, layer, groups, rows, heads, tokens, folded_bias):
    n, d = x.shape
    e = (w.shape[1]) // 4
    dh = e // heads
    assert n == groups * rows and rows % tokens == 0
    steps = rows // tokens
    tok = lambda b, j: (b * steps + j, 0)
    if folded_bias:
        q_shape = jax.ShapeDtypeStruct((groups, heads, 2 * dh, rows), BF16)
        q_spec = pl.BlockSpec((1, heads, 2 * dh, tokens), lambda b, j: (b, 0, 0, j))
        k_shape = jax.ShapeDtypeStruct((groups, heads, rows, 2 * dh), BF16)
        k_spec = pl.BlockSpec((1, heads, tokens, 2 * dh), lambda b, j: (b, 0, j, 0))
        v_shape = jax.ShapeDtypeStruct((groups, heads, dh, rows), BF16)
        v_spec = pl.BlockSpec((1, heads, dh, tokens), lambda b, j: (b, 0, 0, j))
        scratch = [pltpu.VMEM((1, heads), F32)]
        scale = float(dh) ** -0.5 * LOG2E
    else:
        q_shape = k_shape = v_shape = jax.ShapeDtypeStruct((groups, heads, rows, dh), BF16)
        q_spec = k_spec = v_spec = pl.BlockSpec((1, heads, tokens, dh), lambda b, j: (b, 0, j, 0))
        scratch = []
        scale = float(dh) ** -0.5
    k_leaf, v_leaf = kv_leaves
    assert k_leaf.shape == v_leaf.shape == (k_leaf.shape[0], groups, rows * heads, dh)
    leaf_shape = jax.ShapeDtypeStruct(k_leaf.shape, F32)
    leaf_spec = pl.BlockSpec((1, 1, tokens * heads, dh), lambda b, j: (layer, b, j, 0))
    return pl.pallas_call(
        functools.partial(_fox_proj_kernel, scale=scale, folded_bias=folded_bias),
        grid=(groups, steps),
        in_specs=[pl.BlockSpec((tokens, d), tok), _resident(gpre.shape), _resident(w.shape),
                  _resident(wf.shape), _resident(bf.shape),
                  pl.BlockSpec(memory_space=pl.ANY), pl.BlockSpec(memory_space=pl.ANY)],
        out_specs=[q_spec, k_spec, v_spec, leaf_spec, leaf_spec,
                   pl.BlockSpec((tokens, e), tok), pl.BlockSpec((tokens, heads), tok)],
        out_shape=[q_shape, k_shape, v_shape, leaf_shape, leaf_shape,
                   jax.ShapeDtypeStruct((n, e), BF16), jax.ShapeDtypeStruct((n, heads), F32)],
        input_output_aliases={5: 3, 6: 4},
        scratch_shapes=scratch,
        compiler_params=_params(2),
        name="fox_proj",
    )(x, gpre, w, wf, bf, k_leaf, v_leaf)


def _fox_attn_kernel(qt_ref, k_ref, vt_ref, o_ref, s_ref, smax_ref, m_ref, l_ref, acc_ref):
    hpb, tq = qt_ref.shape[1], qt_ref.shape[3]
    tk = s_ref.shape[1]
    qi = pl.program_id(2)

    def produce(j):
        k0 = pl.multiple_of(j * tk, tk)
        for hh in range(hpb):
            s = _dot(k_ref[0, hh, pl.ds(k0, tk), :], qt_ref[0, hh])
            s_ref[hh] = s
            smax_ref[hh] = jnp.max(s, axis=0, keepdims=True)

    def consume(j, diagonal):
        k0 = pl.multiple_of(j * tk, tk)
        if diagonal:
            key = lax.broadcasted_iota(jnp.int32, (tk, tq), 0) + k0
            query = lax.broadcasted_iota(jnp.int32, (tk, tq), 1) + qi * tq
            visible = key <= query
        for hh in range(hpb):
            s = s_ref[hh]
            if diagonal:
                s = jnp.where(visible, s, -jnp.inf)
                s_max = jnp.max(s, axis=0, keepdims=True)
            else:
                s_max = smax_ref[hh]
            m = m_ref[hh]
            m_new = jnp.maximum(m, s_max)
            alpha = jnp.exp2(m - m_new)
            p = jnp.exp2(s - m_new)
            m_ref[hh] = m_new
            l_ref[hh] = alpha * l_ref[hh] + jnp.sum(p, axis=0, keepdims=True)
            acc_ref[hh] = alpha * acc_ref[hh] + _dot(vt_ref[0, hh, :, pl.ds(k0, tk)], p.astype(BF16))

    m_ref[...] = jnp.full_like(m_ref, -jnp.inf)
    l_ref[...] = jnp.zeros_like(l_ref)
    acc_ref[...] = jnp.zeros_like(acc_ref)
    n_full = (qi * tq) // tk
    produce(0)

    @pl.loop(0, n_full // 2)
    def _(i):
        consume(2 * i, False)
        produce(2 * i + 1)
        consume(2 * i + 1, False)
        produce(2 * i + 2)

    @pl.when(n_full % 2 == 1)
    def _():
        consume(n_full - 1, False)
        produce(n_full)

    consume(n_full, True)
    dh = acc_ref.shape[1]
    for hh in range(hpb):
        o_ref[0, :, hh * dh:(hh + 1) * dh] = (acc_ref[hh] / l_ref[hh]).T.astype(BF16)


def _fox_attn(qt, k, vt):
    b, nh, s, dh2 = k.shape
    dh = vt.shape[2]
    t, tk, hpb = ATTN_QUERIES, ATTN_KEYS, ATTN_HEADS
    assert s % tk == 0 and tk % t == 0 and nh % hpb == 0 and dh == LANES and dh2 == 2 * dh
    return pl.pallas_call(
        _fox_attn_kernel,
        grid=(b, nh // hpb, s // t),
        in_specs=[
            pl.BlockSpec((1, hpb, dh2, t), lambda bi, hi, qi: (bi, hi, 0, qi)),
            pl.BlockSpec((1, hpb, s, dh2), lambda bi, hi, qi: (bi, hi, 0, 0)),
            pl.BlockSpec((1, hpb, dh, s), lambda bi, hi, qi: (bi, hi, 0, 0)),
        ],
        out_specs=pl.BlockSpec((1, t, hpb * dh), lambda bi, hi, qi: (bi, qi, hi)),
        out_shape=jax.ShapeDtypeStruct((b, s, nh * dh), BF16),
        scratch_shapes=[pltpu.VMEM((hpb, tk, t), F32), pltpu.VMEM((hpb, 1, t), F32),
                        pltpu.VMEM((hpb, 1, t), F32), pltpu.VMEM((hpb, 1, t), F32),
                        pltpu.VMEM((hpb, dh, t), F32)],
        compiler_params=_params(3),
        name="fox_attn",
    )(qt, k, vt)


def _fox_sample_attn_kernel(q_ref, kc_ref, vc_ref, kn_ref, vn_ref, lfct_ref, lfn_ref, lfnt_ref,
                            o_ref, cct_ref, cnt_ref, cn_ref, m_ref, l_ref, acc_ref, s_ref, sn_ref, pv_ref):
    nh, t, dh = q_ref.shape[1], q_ref.shape[2], q_ref.shape[3]
    past = lfct_ref.shape[2]
    pad = lfnt_ref.shape[2]
    pc = kc_ref.shape[2] // nh
    step = pl.program_id(1)

    @pl.when(step == 0)
    def _():
        triu = _tri(CUMSUM_CHUNK, upper=True)
        total = jnp.zeros((nh, 1), F32)
        for i in range(past // CUMSUM_CHUNK):
            cs = slice(i * CUMSUM_CHUNK, (i + 1) * CUMSUM_CHUNK)
            blk = _dot_ones_rhs(lfct_ref[0, :, cs], triu) + total
            cct_ref[:, cs] = blk
            total = blk[:, CUMSUM_CHUNK - 1:CUMSUM_CHUNK]
        eye = (lax.broadcasted_iota(jnp.int32, (nh, nh), 0)
               == lax.broadcasted_iota(jnp.int32, (nh, nh), 1))
        total_row = jnp.sum(jnp.where(eye, total, 0.0), axis=0, keepdims=True)
        cn_ref[...] = _dot_ones_lhs(_tri(t, upper=False), lfn_ref[0]) + total_row
        cnt_ref[...] = _dot_ones_rhs(lfnt_ref[0], _tri(pad, upper=True)) + total
        m_ref[...] = jnp.full_like(m_ref, -jnp.inf)
        l_ref[...] = jnp.zeros_like(l_ref)
        acc_ref[...] = jnp.zeros_like(acc_ref)

    def rows(hd):
        return slice(hd * t, (hd + 1) * t)

    def attend(s, values):
        m = m_ref[...]
        m_new = jnp.maximum(m, jnp.max(s, axis=1, keepdims=True))
        alpha = jnp.exp(m - m_new)
        p = jnp.exp(s - m_new)
        m_ref[...] = m_new
        l_ref[...] = alpha * l_ref[...] + jnp.sum(p, axis=1, keepdims=True)
        p = p.astype(BF16)
        for hd in range(nh):
            pv_ref[rows(hd), :] = _dot(p[rows(hd), :], values(hd))
        acc_ref[...] = alpha * acc_ref[...] + pv_ref[...]

    p0 = pl.multiple_of(step * pc, pc)
    for hd in range(nh):
        kc = kc_ref.at[0, 0][pl.ds(hd, pc, stride=nh), :].astype(BF16)
        s_ref[rows(hd), :] = (lax.dot_general(q_ref[0, hd], kc, _NT, preferred_element_type=F32)
                              + cn_ref[:, hd:hd + 1] - cct_ref[hd:hd + 1, pl.ds(p0, pc)])
    attend(s_ref[...], lambda hd: vc_ref.at[0, 0][pl.ds(hd, pc, stride=nh), :].astype(BF16))

    @pl.when(step == pl.num_programs(1) - 1)
    def _():
        visible = (lax.broadcasted_iota(jnp.int32, (t, pad), 1)
                   <= lax.broadcasted_iota(jnp.int32, (t, pad), 0))
        zeros = jnp.zeros((pad - t, dh), BF16)
        for hd in range(nh):
            kn = jnp.concatenate([kn_ref[0, hd], zeros], axis=0)
            s = (lax.dot_general(q_ref[0, hd], kn, _NT, preferred_element_type=F32)
                 + cn_ref[:, hd:hd + 1] - cnt_ref[hd:hd + 1, :])
            sn_ref[rows(hd), :] = jnp.where(visible, s, -jnp.inf)
        attend(sn_ref[...], lambda hd: jnp.concatenate([vn_ref[0, hd], zeros], axis=0))
        o = (acc_ref[...] / l_ref[...]).astype(BF16)
        for hd in range(nh):
            o_ref[:, hd * dh:(hd + 1) * dh] = o[rows(hd), :]


def _fox_sample_attn(q, kn, vn, k_cache, v_cache, layer, logf_cache_t, logf_new, logf_new_t, *, batch):
    _, nh, n, dh = q.shape
    t = n // batch
    past = k_cache.shape[2] // nh
    pc = CACHE_CHUNK
    assert past % pc == 0 and past % CUMSUM_CHUNK == 0 and dh == LANES
    new_spec = pl.BlockSpec((1, nh, t, dh), lambda bi, si: (0, 0, bi, 0))
    cache_spec = pl.BlockSpec((1, 1, pc * nh, dh), lambda bi, si: (layer, bi, si, 0))
    whole = lambda shape: pl.BlockSpec((1,) + shape[1:], lambda bi, si: (bi, 0, 0))
    pad = logf_new_t.shape[2]
    return pl.pallas_call(
        _fox_sample_attn_kernel,
        grid=(batch, past // pc),
        in_specs=[new_spec, cache_spec, cache_spec, new_spec, new_spec,
                  whole(logf_cache_t.shape), whole(logf_new.shape), whole(logf_new_t.shape)],
        out_specs=pl.BlockSpec((t, nh * dh), lambda bi, si: (bi, 0)),
        out_shape=jax.ShapeDtypeStruct((n, nh * dh), BF16),
        scratch_shapes=[pltpu.VMEM((nh, past), F32), pltpu.VMEM((nh, pad), F32),
                        pltpu.VMEM((t, nh), F32), pltpu.VMEM((nh * t, 1), F32),
                        pltpu.VMEM((nh * t, 1), F32), pltpu.VMEM((nh * t, dh), F32),
                        pltpu.VMEM((nh * t, pc), F32), pltpu.VMEM((nh * t, pad), F32),
                        pltpu.VMEM((nh * t, dh), F32)],
        compiler_params=_params(2),
        name="fox_sample_attn",
    )(q, k_cache, v_cache, kn, vn, logf_cache_t, logf_new, logf_new_t)


def _fox_post_kernel(o_ref, sz_ref, x_ref, p_ref, gpost_ref, wout_ref, wgate_ref, wproj_ref, out_ref):
    o = _dot(o_ref[...] * sz_ref[...], wout_ref[...])
    out_ref[...] = _residual_and_embedding(x_ref[...], o, p_ref[...], gpost_ref, wgate_ref, wproj_ref)


def _fox_post(o, sz, x, p, gpost, w_out, w_gate, w_proj, *, tokens):
    n, d = x.shape
    e = o.shape[1]
    assert n % tokens == 0
    row = lambda i: (i, 0)
    return pl.pallas_call(
        _fox_post_kernel,
        grid=(n // tokens,),
        in_specs=[pl.BlockSpec((tokens, e), row), pl.BlockSpec((tokens, e), row),
                  pl.BlockSpec((tokens, d), row), pl.BlockSpec((tokens, p.shape[1]), row),
                  _resident(gpost.shape), _resident(w_out.shape), _resident(w_gate.shape),
                  _resident(w_proj.shape)],
        out_specs=pl.BlockSpec((tokens, d), row),
        out_shape=jax.ShapeDtypeStruct((n, d), F32),
        compiler_params=_params(1),
        name="fox_post",
    )(o, sz, x, p, gpost, w_out, w_gate, w_proj)


def _mix_operands(w_s, b_s, rows, gd):
    n = min(rows, GMLP_BLOCK)
    reps = GMLP_BLOCK // n
    w = jnp.tile(w_s[:, :n, :n], (1, reps, reps))
    b = jnp.repeat(jnp.tile(b_s[:, :n], (1, reps)).T, gd, axis=1)
    return w, b, n


def kernel(x_prompt, x_sample, cache_fox_k, cache_fox_v, cache_fox_logf, p_prompt, p_sample, norm_pre, norm_post, gmlp_w_in, gmlp_ln_g, gmlp_ln_b, gmlp_w_s, gmlp_b_s, gmlp_w_out, fox_w_in, fox_b_f, fox_w_out, ple_w_proj, ple_w_gate):
    bp, sp, d = x_prompt.shape
    bs, ts, _ = x_sample.shape
    depth = norm_pre.shape[0]
    n_fox, _, past, nh, dh = cache_fox_k.shape
    e = nh * dh
    groups = gmlp_w_s.shape[1]
    gd = e // groups
    n_p, n_s = bp * sp, bs * ts

    xp = x_prompt.reshape(n_p, d)
    xs = x_sample.reshape(n_s, d)
    cache_k = cache_fox_k.reshape(cache_fox_k.shape[0], bs, past * nh, dh)
    cache_v = cache_fox_v.reshape(cache_fox_v.shape[0], bs, past * nh, dh)
    gmlp_v_s = []
    flf_p, flf_s = [], []
    kv_p = (jnp.zeros((n_fox, bp, sp * nh, dh), F32),) * 2
    kv_s = (jnp.zeros((n_fox, 1, n_s * nh, dh), F32),) * 2
    for i in range(depth):
        j = i // 2
        gpre, gpost = norm_pre[i][None], norm_post[i][None]
        w_gate, w_proj = ple_w_gate[i].astype(BF16), ple_w_proj[i].astype(BF16)
        pp = p_prompt[i].reshape(n_p, -1)
        ps = p_sample[i].reshape(n_s, -1)
        if i % 2 == 0:
            w_in, w_out = gmlp_w_in[j].astype(BF16), gmlp_w_out[j].astype(BF16)
            ln_g, ln_b = gmlp_ln_g[j][None], gmlp_ln_b[j][None]
            wm_p, bm_p, _ = _mix_operands(gmlp_w_s[j], gmlp_b_s[j], sp, gd)
            wm_s, bm_s, n_mix = _mix_operands(gmlp_w_s[j], gmlp_b_s[j], ts, gd)
            xp, _ = _gmlp_layer(xp, pp, gpre, gpost, w_in, ln_g, ln_b, wm_p, bm_p, w_out, w_gate, w_proj,
                                tokens=GMLP_TOKENS, span=CHUNK, causal=True, emit_v=False)
            assert n_mix <= CHUNK and n_s % GMLP_BLOCK == 0
            xs, v_s = _gmlp_layer(xs, ps, gpre, gpost, w_in, ln_g, ln_b, wm_s, bm_s, w_out, w_gate, w_proj,
                                  tokens=GMLP_BLOCK, span=n_mix, causal=False, emit_v=True)
            gmlp_v_s.append(v_s.reshape(bs, ts, e))
        else:
            w = fox_w_in[j][:, :4 * e].astype(BF16)
            wf = fox_w_in[j][:, 4 * e:].astype(BF16)
            bf = fox_b_f[j][None]
            w_out = fox_w_out[j].astype(BF16)

            q, kb, vt, *kv_p, sz, lf = _fox_proj(xp, gpre, w, wf, bf, kv_p, layer=j, groups=bp, rows=sp,
                                                 heads=nh, tokens=PROJ_TOKENS, folded_bias=True)
            o = _fox_attn(q, kb, vt)
            xp = _fox_post(o.reshape(n_p, e), sz, xp, pp, gpost, w_out, w_gate, w_proj, tokens=POST_TOKENS)
            flf_p.append(lf.reshape(bp, sp, nh))

            q, kb, vb, *kv_s, sz, lf = _fox_proj(xs, gpre, w, wf, bf, kv_s, layer=j, groups=1, rows=n_s,
                                                 heads=nh, tokens=n_s, folded_bias=False)
            lf3 = lf.reshape(bs, ts, nh)
            lf3_t = jnp.pad(lf3.transpose(0, 2, 1), ((0, 0), (0, 0), (0, LANES - ts)))
            o = _fox_sample_attn(q, kb, vb, cache_k, cache_v, j, cache_fox_logf[j].transpose(0, 2, 1),
                                 lf3, lf3_t, batch=bs)
            xs = _fox_post(o, sz, xs, ps, gpost, w_out, w_gate, w_proj, tokens=n_s)
            flf_s.append(lf3)
    fk_p, fv_p = (a.reshape(n_fox, bp, sp, nh, dh) for a in kv_p)
    fk_s, fv_s = (a.reshape(n_fox, bs, ts, nh, dh) for a in kv_s)
    return (xp.reshape(bp, sp, d), xs.reshape(bs, ts, d), jnp.stack(gmlp_v_s),
            fk_p, fv_p, jnp.stack(flf_p), fk_s, fv_s, jnp.stack(flf_s))
```

```python
import functools

import jax
import jax.numpy as jnp
from jax import lax
from jax.experimental import pallas as pl
from jax.experimental.pallas import tpu as pltpu

F32 = jnp.float32
BF16 = jnp.bfloat16

CHUNK = 64
GMLP_BLOCK = 128
RMS_EPS = 1e-6
LN_EPS = 1e-5
LANES = 128
VMEM_LIMIT_BYTES = 56 * 1024 * 1024

GMLP_TOKENS = 256
PROJ_TOKENS = 256
POST_TOKENS = 512
ATTN_QUERIES = 512
ATTN_KEYS = 512
ATTN_HEADS = 4
CACHE_CHUNK = 256
LOG2E = 1.4426950408889634
CUMSUM_CHUNK = 256

_NT = (((1,), (1,)), ((), ()))


def _params(n_axes):
    return pltpu.CompilerParams(
        dimension_semantics=("arbitrary",) * n_axes, vmem_limit_bytes=VMEM_LIMIT_BYTES)


def _resident(shape):
    zeros = (0,) * len(shape)
    return pl.BlockSpec(shape, lambda *_: zeros, pipeline_mode=pl.Buffered(1))


def _dot(a, b):
    return jnp.dot(a, b, preferred_element_type=F32)


def _rms_norm(x, g):
    return x * lax.rsqrt(jnp.mean(x * x, axis=-1, keepdims=True) + RMS_EPS) * g


def _silu(z):
    return z * jax.nn.sigmoid(z)


def _split3(x):
    a = x.astype(BF16)
    r = x - a.astype(F32)
    b = r.astype(BF16)
    c = (r - b.astype(F32)).astype(BF16)
    return a, b, c


def _dot_ones_lhs(ones, x):
    a, b, c = _split3(x)
    return (_dot(ones, a) + _dot(ones, b)) + _dot(ones, c)


def _dot_ones_rhs(x, ones):
    a, b, c = _split3(x)
    return (_dot(a, ones) + _dot(b, ones)) + _dot(c, ones)


def _tri(n, upper):
    r = lax.broadcasted_iota(jnp.int32, (n, n), 0)
    c = lax.broadcasted_iota(jnp.int32, (n, n), 1)
    keep = (r <= c) if upper else (c <= r)
    return jnp.where(keep, 1.0, 0.0).astype(BF16)


def _residual_and_embedding(x, o, p, gpost_ref, wgate_ref, wproj_ref):
    x1 = x + _rms_norm(o, gpost_ref[...])
    gate = jax.nn.sigmoid(_dot(x1.astype(BF16), wgate_ref[...]))
    return x1 + gate * _dot(p.astype(BF16), wproj_ref[...])


def _gmlp_layer_kernel(x_ref, p_ref, gpre_ref, gpost_ref, win_ref, lng_ref, lnb_ref,
                       wmix_ref, bmix_ref, wout_ref, wgate_ref, wproj_ref, *rest,
                       span_shift, causal, emit_v):
    if emit_v:
        out_ref, v_ref, y_ref = rest
    else:
        out_ref, y_ref = rest
    tm = x_ref.shape[0]
    e = wout_ref.shape[0]
    groups, blk, _ = wmix_ref.shape
    gd = e // groups

    x = x_ref[...]
    h = _rms_norm(x, gpre_ref[...]).astype(BF16)
    u = jax.nn.gelu(_dot(h, win_ref[:, 0:e]))
    gv = jax.nn.gelu(_dot(h, win_ref[:, e:2 * e]))
    sz = _silu(_dot(h, win_ref[:, 2 * e:3 * e]))
    d = gv - jnp.mean(gv, axis=-1, keepdims=True)
    v = d * lax.rsqrt(jnp.mean(d * d, axis=-1, keepdims=True) + LN_EPS) * lng_ref[...] + lnb_ref[...]
    if emit_v:
        v_ref[...] = v
    vb = v.astype(BF16)

    qs = lax.shift_right_logical(lax.broadcasted_iota(jnp.int32, (blk, blk), 0), span_shift)
    ks = lax.shift_right_logical(lax.broadcasted_iota(jnp.int32, (blk, blk), 1), span_shift)
    allowed = (ks <= qs) if causal else (ks == qs)
    row_blocks = [slice(r * blk, (r + 1) * blk) for r in range(tm // blk)]
    for g in range(groups):
        cs = slice(g * gd, (g + 1) * gd)
        wm = jnp.where(allowed, wmix_ref[g], 0.0).astype(BF16)
        mixed = _dot(wm, jnp.concatenate([vb[rs, cs] for rs in row_blocks], axis=1))
        for r, rs in enumerate(row_blocks):
            s = mixed[:, r * gd:(r + 1) * gd] + bmix_ref[:, cs]
            y_ref[rs, cs] = (u[rs, cs] * s * sz[rs, cs]).astype(BF16)
    o = _dot(y_ref[...], wout_ref[...])
    out_ref[...] = _residual_and_embedding(x, o, p_ref[...], gpost_ref, wgate_ref, wproj_ref)


def _gmlp_layer(x, p, gpre, gpost, w_in, ln_g, ln_b, w_mix, b_mix, w_out, w_gate, w_proj,
                *, layer, tokens, span, causal, emit_v):
    n, d = x.shape
    e = w_out.shape[0]
    pd = p.shape[2]
    assert n % tokens == 0 and tokens % GMLP_BLOCK == 0 and span & (span - 1) == 0
    row = lambda i: (i, 0)
    out_shape = [jax.ShapeDtypeStruct((n, d), F32)]
    out_specs = [pl.BlockSpec((tokens, d), row)]
    if emit_v:
        out_shape.append(jax.ShapeDtypeStruct((n, e), F32))
        out_specs.append(pl.BlockSpec((tokens, e), row))
    res = pl.pallas_call(
        functools.partial(_gmlp_layer_kernel, span_shift=span.bit_length() - 1,
                          causal=causal, emit_v=emit_v),
        grid=(n // tokens,),
        in_specs=[
            pl.BlockSpec((tokens, d), row),
            pl.BlockSpec((None, tokens, pd), lambda i: (layer, i, 0)),
            _resident(gpre.shape), _resident(gpost.shape), _resident(w_in.shape),
            _resident(ln_g.shape), _resident(ln_b.shape), _resident(w_mix.shape),
            _resident(b_mix.shape), _resident(w_out.shape), _resident(w_gate.shape),
            _resident(w_proj.shape),
        ],
        out_specs=out_specs,
        out_shape=out_shape,
        scratch_shapes=[pltpu.VMEM((tokens, e), BF16)],
        compiler_params=_params(1),
        name="gmlp_layer",
    )(x, p, gpre, gpost, w_in, ln_g, ln_b, w_mix, b_mix, w_out, w_gate, w_proj)
    return res if emit_v else (res[0], None)


def _bias_columns(c, hd, width):
    hi, mid, lo = _split3(c[:, hd:hd + 1])
    shape = (c.shape[0], width)
    return tuple(jnp.broadcast_to(t.astype(F32), shape) for t in (hi, mid, lo))


def _fox_proj_kernel(x_ref, gpre_ref, w_ref, wf_ref, bf_ref, kf_in_ref, vf_in_ref,
                     q_ref, k_ref, v_ref, kf_ref, vf_ref, sz_ref, logf_ref, *scratch, scale, folded_bias):
    del kf_in_ref, vf_in_ref
    tm, e = sz_ref.shape
    nh = q_ref.shape[1]
    dh = e // nh
    heads = [(hd, slice(hd * dh, (hd + 1) * dh)) for hd in range(nh)]
    h = _rms_norm(x_ref[...], gpre_ref[...]).astype(BF16)
    logf = jax.nn.log_sigmoid(_dot(h, wf_ref[...]) + bf_ref[...])
    logf_ref[...] = logf

    if folded_bias:
        (carry_ref,) = scratch

        @pl.when(pl.program_id(1) == 0)
        def _():
            carry_ref[...] = jnp.zeros_like(carry_ref)

        c = _dot_ones_lhs(_tri(tm, upper=False), logf) + carry_ref[...]
        carry_ref[...] = c[tm - 1:tm, :]
        c = c * LOG2E
        lane = lax.broadcasted_iota(jnp.int32, (tm, dh), 1)
        third = lane - jnp.where(lane >= 3, 3, 0)
        in_q, in_k = lane < 3, (lane >= 3) & (lane < 6)
        ones_q = jnp.where(in_k, 1.0, 0.0)
        ones_k = jnp.where(in_q, 1.0, 0.0)
        for hd, _ in heads:
            hi, mid, lo = _bias_columns(c, hd, dh)
            terms = jnp.where(third == 0, hi, jnp.where(third == 1, mid, lo))
            q_ref[0, hd, dh:2 * dh, :] = jnp.where(in_q, terms, ones_q).T.astype(BF16)
            k_ref[0, hd, :, dh:2 * dh] = jnp.where(in_k, -terms, ones_k).astype(BF16)

    q = _dot(h, w_ref[:, 0:e]) * scale
    for hd, hs in heads:
        if folded_bias:
            q_ref[0, hd, 0:dh, :] = q[:, hs].T.astype(BF16)
        else:
            q_ref[0, hd] = q[:, hs].astype(BF16)
    k = _dot(h, w_ref[:, e:2 * e])
    for hd, hs in heads:
        kf_ref.at[0, 0][pl.ds(hd, tm, stride=nh), :] = k[:, hs]
        if folded_bias:
            k_ref[0, hd, :, 0:dh] = k[:, hs].astype(BF16)
        else:
            k_ref[0, hd] = k[:, hs].astype(BF16)
    v = _dot(h, w_ref[:, 2 * e:3 * e])
    for hd, hs in heads:
        vf_ref.at[0, 0][pl.ds(hd, tm, stride=nh), :] = v[:, hs]
        v_ref[0, hd] = (v[:, hs].T if folded_bias else v[:, hs]).astype(BF16)
    sz_ref[...] = _silu(_dot(h, w_ref[:, 3 * e:4 * e])).astype(BF16)


def _fox_proj(x, gpre, w, wf, bf, kv_leaves, *, layer, groups, rows, heads, tokens, folded_bias):
    n, d = x.shape
    e = (w.shape[1]) // 4
    dh = e // heads
    assert n == groups * rows and rows % tokens == 0
    steps = rows // tokens
    tok = lambda b, j: (b * steps + j, 0)
    if folded_bias:
        q_shape = jax.ShapeDtypeStruct((groups, heads, 2 * dh, rows), BF16)
        q_spec = pl.BlockSpec((1, heads, 2 * dh, tokens), lambda b, j: (b, 0, 0, j))
        k_shape = jax.ShapeDtypeStruct((groups, heads, rows, 2 * dh), BF16)
        k_spec = pl.BlockSpec((1, heads, tokens, 2 * dh), lambda b, j: (b, 0, j, 0))
        v_shape = jax.ShapeDtypeStruct((groups, heads, dh, rows), BF16)
        v_spec = pl.BlockSpec((1, heads, dh, tokens), lambda b, j: (b, 0, 0, j))
        scratch = [pltpu.VMEM((1, heads), F32)]
        scale = float(dh) ** -0.5 * LOG2E
    else:
        q_shape = k_shape = v_shape = jax.ShapeDtypeStruct((groups, heads, rows, dh), BF16)
        q_spec = k_spec = v_spec = pl.BlockSpec((1, heads, tokens, dh), lambda b, j: (b, 0, j, 0))
        scratch = []
        scale = float(dh) ** -0.5
    k_leaf, v_leaf = kv_leaves
    assert k_leaf.shape == v_leaf.shape == (k_leaf.shape[0], groups, rows * heads, dh)
    leaf_shape = jax.ShapeDtypeStruct(k_leaf.shape, F32)
    leaf_spec = pl.BlockSpec((1, 1, tokens * heads, dh), lambda b, j: (layer, b, j, 0))
    return pl.pallas_call(
        functools.partial(_fox_proj_kernel, scale=scale, folded_bias=folded_bias),
        grid=(groups, steps),
        in_specs=[pl.BlockSpec((tokens, d), tok), _resident(gpre.shape), _resident(w.shape),
                  _resident(wf.shape), _resident(bf.shape),
                  pl.BlockSpec(memory_space=pl.ANY), pl.BlockSpec(memory_space=pl.ANY)],
        out_specs=[q_spec, k_spec, v_spec, leaf_spec, leaf_spec,
                   pl.BlockSpec((tokens, e), tok), pl.BlockSpec((tokens, heads), tok)],
        out_shape=[q_shape, k_shape, v_shape, leaf_shape, leaf_shape,
                   jax.ShapeDtypeStruct((n, e), BF16), jax.ShapeDtypeStruct((n, heads), F32)],
        input_output_aliases={5: 3, 6: 4},
        scratch_shapes=scratch,
        compiler_params=_params(2),
        name="fox_proj",
    )(x, gpre, w, wf, bf, k_leaf, v_leaf)


def _fox_attn_kernel(qt_ref, k_ref, vt_ref, o_ref, s_ref, smax_ref, m_ref, l_ref, acc_ref):
    hpb, tq = qt_ref.shape[1], qt_ref.shape[3]
    tk = s_ref.shape[2]
    qi = pl.program_id(2)

    def produce(j, slot, hh):
        k0 = pl.multiple_of(j * tk, tk)
        s = _dot(k_ref[0, hh, pl.ds(k0, tk), :], qt_ref[0, hh])
        s_ref[slot, hh] = s
        smax_ref[slot, hh] = jnp.max(s, axis=0, keepdims=True)

    def consume(j, slot, hh, diagonal):
        k0 = pl.multiple_of(j * tk, tk)
        s = s_ref[slot, hh]
        if diagonal:
            key = lax.broadcasted_iota(jnp.int32, (tk, tq), 0) + k0
            query = lax.broadcasted_iota(jnp.int32, (tk, tq), 1) + qi * tq
            s = jnp.where(key <= query, s, -jnp.inf)
            s_max = jnp.max(s, axis=0, keepdims=True)
        else:
            s_max = smax_ref[slot, hh]
        m = m_ref[hh]
        m_new = jnp.maximum(m, s_max)
        alpha = jnp.exp2(m - m_new)
        p = jnp.exp2(s - m_new)
        m_ref[hh] = m_new
        l_ref[hh] = alpha * l_ref[hh] + jnp.sum(p, axis=0, keepdims=True)
        acc_ref[hh] = alpha * acc_ref[hh] + _dot(vt_ref[0, hh, :, pl.ds(k0, tk)], p.astype(BF16))

    def step(produced, consumed):
        for hh in range(hpb):
            if produced is not None:
                produce(*produced, hh)
            if consumed is not None:
                consume(*consumed[:2], hh, consumed[2])

    m_ref[...] = jnp.full_like(m_ref, -jnp.inf)
    l_ref[...] = jnp.zeros_like(l_ref)
    acc_ref[...] = jnp.zeros_like(acc_ref)
    n_full = (qi * tq) // tk
    step((0, 0), None)

    @pl.loop(0, n_full // 2)
    def _(i):
        step((2 * i + 1, 1), (2 * i, 0, False))
        step((2 * i + 2, 0), (2 * i + 1, 1, False))

    @pl.when(n_full % 2 == 1)
    def _():
        step((n_full, 1), (n_full - 1, 0, False))
        step(None, (n_full, 1, True))

    @pl.when(n_full % 2 == 0)
    def _():
        step(None, (n_full, 0, True))

    dh = acc_ref.shape[1]
    for hh in range(hpb):
        o_ref[0, :, hh * dh:(hh + 1) * dh] = (acc_ref[hh] / l_ref[hh]).T.astype(BF16)


def _fox_attn(qt, k, vt):
    b, nh, s, dh2 = k.shape
    dh = vt.shape[2]
    t, tk, hpb = ATTN_QUERIES, ATTN_KEYS, ATTN_HEADS
    assert s % tk == 0 and tk % t == 0 and nh % hpb == 0 and dh == LANES and dh2 == 2 * dh
    return pl.pallas_call(
        _fox_attn_kernel,
        grid=(b, nh // hpb, s // t),
        in_specs=[
            pl.BlockSpec((1, hpb, dh2, t), lambda bi, hi, qi: (bi, hi, 0, qi)),
            pl.BlockSpec((1, hpb, s, dh2), lambda bi, hi, qi: (bi, hi, 0, 0)),
            pl.BlockSpec((1, hpb, dh, s), lambda bi, hi, qi: (bi, hi, 0, 0)),
        ],
        out_specs=pl.BlockSpec((1, t, hpb * dh), lambda bi, hi, qi: (bi, qi, hi)),
        out_shape=jax.ShapeDtypeStruct((b, s, nh * dh), BF16),
        scratch_shapes=[pltpu.VMEM((2, hpb, tk, t), F32), pltpu.VMEM((2, hpb, 1, t), F32),
                        pltpu.VMEM((hpb, 1, t), F32), pltpu.VMEM((hpb, 1, t), F32),
                        pltpu.VMEM((hpb, dh, t), F32)],
        compiler_params=_params(3),
        name="fox_attn",
    )(qt, k, vt)


def _fox_sample_attn_kernel(q_ref, kc_ref, vc_ref, kn_ref, vn_ref, lfct_ref, lfn_ref, lfnt_ref,
                            o_ref, cct_ref, cnt_ref, cn_ref, m_ref, l_ref, acc_ref, s_ref, sn_ref, pv_ref):
    nh, t, dh = q_ref.shape[1], q_ref.shape[2], q_ref.shape[3]
    past = lfct_ref.shape[2]
    pad = lfnt_ref.shape[2]
    pc = kc_ref.shape[2] // nh
    step = pl.program_id(1)

    @pl.when(step == 0)
    def _():
        triu = _tri(CUMSUM_CHUNK, upper=True)
        total = jnp.zeros((nh, 1), F32)
        for i in range(past // CUMSUM_CHUNK):
            cs = slice(i * CUMSUM_CHUNK, (i + 1) * CUMSUM_CHUNK)
            blk = _dot_ones_rhs(lfct_ref[0, :, cs], triu) + total
            cct_ref[:, cs] = blk
            total = blk[:, CUMSUM_CHUNK - 1:CUMSUM_CHUNK]
        eye = (lax.broadcasted_iota(jnp.int32, (nh, nh), 0)
               == lax.broadcasted_iota(jnp.int32, (nh, nh), 1))
        total_row = jnp.sum(jnp.where(eye, total, 0.0), axis=0, keepdims=True)
        cn_ref[...] = _dot_ones_lhs(_tri(t, upper=False), lfn_ref[0]) + total_row
        cnt_ref[...] = _dot_ones_rhs(lfnt_ref[0], _tri(pad, upper=True)) + total
        m_ref[...] = jnp.full_like(m_ref, -jnp.inf)
        l_ref[...] = jnp.zeros_like(l_ref)
        acc_ref[...] = jnp.zeros_like(acc_ref)

    def rows(hd):
        return slice(hd * t, (hd + 1) * t)

    def attend(s, values):
        m = m_ref[...]
        m_new = jnp.maximum(m, jnp.max(s, axis=1, keepdims=True))
        alpha = jnp.exp(m - m_new)
        p = jnp.exp(s - m_new)
        m_ref[...] = m_new
        l_ref[...] = alpha * l_ref[...] + jnp.sum(p, axis=1, keepdims=True)
        p = p.astype(BF16)
        for hd in range(nh):
            pv_ref[rows(hd), :] = _dot(p[rows(hd), :], values(hd))
        acc_ref[...] = alpha * acc_ref[...] + pv_ref[...]

    p0 = pl.multiple_of(step * pc, pc)
    for hd in range(nh):
        kc = kc_ref.at[0, 0][pl.ds(hd, pc, stride=nh), :].astype(BF16)
        s_ref[rows(hd), :] = (lax.dot_general(q_ref[0, hd], kc, _NT, preferred_element_type=F32)
                              + cn_ref[:, hd:hd + 1] - cct_ref[hd:hd + 1, pl.ds(p0, pc)])
    attend(s_ref[...], lambda hd: vc_ref.at[0, 0][pl.ds(hd, pc, stride=nh), :].astype(BF16))

    @pl.when(step == pl.num_programs(1) - 1)
    def _():
        visible = (lax.broadcasted_iota(jnp.int32, (t, pad), 1)
                   <= lax.broadcasted_iota(jnp.int32, (t, pad), 0))
        zeros = jnp.zeros((pad - t, dh), BF16)
        for hd in range(nh):
            kn = jnp.concatenate([kn_ref[0, hd], zeros], axis=0)
            s = (lax.dot_general(q_ref[0, hd], kn, _NT, preferred_element_type=F32)
                 + cn_ref[:, hd:hd + 1] - cnt_ref[hd:hd + 1, :])
            sn_ref[rows(hd), :] = jnp.where(visible, s, -jnp.inf)
        attend(sn_ref[...], lambda hd: jnp.concatenate([vn_ref[0, hd], zeros], axis=0))
        o = (acc_ref[...] / l_ref[...]).astype(BF16)
        for hd in range(nh):
            o_ref[:, hd * dh:(hd + 1) * dh] = o[rows(hd), :]


def _fox_sample_attn(q, kn, vn, k_cache, v_cache, layer, logf_cache_t, logf_new, logf_new_t, *, batch):
    _, nh, n, dh = q.shape
    t = n // batch
    past = k_cache.shape[2] // nh
    pc = CACHE_CHUNK
    assert past % pc == 0 and past % CUMSUM_CHUNK == 0 and dh == LANES
    new_spec = pl.BlockSpec((1, nh, t, dh), lambda bi, si: (0, 0, bi, 0))
    cache_spec = pl.BlockSpec((1, 1, pc * nh, dh), lambda bi, si: (layer, bi, si, 0))
    whole = lambda shape: pl.BlockSpec((1,) + shape[1:], lambda bi, si: (bi, 0, 0))
    pad = logf_new_t.shape[2]
    return pl.pallas_call(
        _fox_sample_attn_kernel,
        grid=(batch, past // pc),
        in_specs=[new_spec, cache_spec, cache_spec, new_spec, new_spec,
                  whole(logf_cache_t.shape), whole(logf_new.shape), whole(logf_new_t.shape)],
        out_specs=pl.BlockSpec((t, nh * dh), lambda bi, si: (bi, 0)),
        out_shape=jax.ShapeDtypeStruct((n, nh * dh), BF16),
        scratch_shapes=[pltpu.VMEM((nh, past), F32), pltpu.VMEM((nh, pad), F32),
                        pltpu.VMEM((t, nh), F32), pltpu.VMEM((nh * t, 1), F32),
                        pltpu.VMEM((nh * t, 1), F32), pltpu.VMEM((nh * t, dh), F32),
                        pltpu.VMEM((nh * t, pc), F32), pltpu.VMEM((nh * t, pad), F32),
                        pltpu.VMEM((nh * t, dh), F32)],
        compiler_params=_params(2),
        name="fox_sample_attn",
    )(q, k_cache, v_cache, kn, vn, logf_cache_t, logf_new, logf_new_t)


def _fox_post_kernel(o_ref, sz_ref, x_ref, p_ref, gpost_ref, wout_ref, wgate_ref, wproj_ref, out_ref):
    o = _dot(o_ref[...] * sz_ref[...], wout_ref[...])
    out_ref[...] = _residual_and_embedding(x_ref[...], o, p_ref[...], gpost_ref, wgate_ref, wproj_ref)


def _fox_post(o, sz, x, p, gpost, w_out, w_gate, w_proj, *, layer, tokens):
    n, d = x.shape
    e = o.shape[1]
    assert n % tokens == 0
    row = lambda i: (i, 0)
    return pl.pallas_call(
        _fox_post_kernel,
        grid=(n // tokens,),
        in_specs=[pl.BlockSpec((tokens, e), row), pl.BlockSpec((tokens, e), row),
                  pl.BlockSpec((tokens, d), row),
                  pl.BlockSpec((None, tokens, p.shape[2]), lambda i: (layer, i, 0)),
                  _resident(gpost.shape), _resident(w_out.shape), _resident(w_gate.shape),
                  _resident(w_proj.shape)],
        out_specs=pl.BlockSpec((tokens, d), row),
        out_shape=jax.ShapeDtypeStruct((n, d), F32),
        compiler_params=_params(1),
        name="fox_post",
    )(o, sz, x, p, gpost, w_out, w_gate, w_proj)


def _mix_operands(w_s, b_s, rows, gd):
    n = min(rows, GMLP_BLOCK)
    reps = GMLP_BLOCK // n
    w = jnp.tile(w_s[:, :n, :n], (1, reps, reps))
    b = jnp.repeat(jnp.tile(b_s[:, :n], (1, reps)).T, gd, axis=1)
    return w, b, n


def kernel(x_prompt, x_sample, cache_fox_k, cache_fox_v, cache_fox_logf, p_prompt, p_sample, norm_pre, norm_post, gmlp_w_in, gmlp_ln_g, gmlp_ln_b, gmlp_w_s, gmlp_b_s, gmlp_w_out, fox_w_in, fox_b_f, fox_w_out, ple_w_proj, ple_w_gate):
    bp, sp, d = x_prompt.shape
    bs, ts, _ = x_sample.shape
    depth = norm_pre.shape[0]
    n_fox, _, past, nh, dh = cache_fox_k.shape
    e = nh * dh
    groups = gmlp_w_s.shape[1]
    gd = e // groups
    n_p, n_s = bp * sp, bs * ts

    xp = x_prompt.reshape(n_p, d)
    xs = x_sample.reshape(n_s, d)
    cache_k = cache_fox_k.reshape(cache_fox_k.shape[0], bs, past * nh, dh)
    cache_v = cache_fox_v.reshape(cache_fox_v.shape[0], bs, past * nh, dh)
    pp = p_prompt.reshape(depth, n_p, -1)
    ps = p_sample.reshape(depth, n_s, -1)
    gmlp_v_s = []
    flf_p, flf_s = [], []
    kv_p = (jnp.zeros((n_fox, bp, sp * nh, dh), F32),) * 2
    kv_s = (jnp.zeros((n_fox, 1, n_s * nh, dh), F32),) * 2
    for i in range(depth):
        j = i // 2
        gpre, gpost = norm_pre[i][None], norm_post[i][None]
        w_gate, w_proj = ple_w_gate[i].astype(BF16), ple_w_proj[i].astype(BF16)
        if i % 2 == 0:
            w_in, w_out = gmlp_w_in[j].astype(BF16), gmlp_w_out[j].astype(BF16)
            ln_g, ln_b = gmlp_ln_g[j][None], gmlp_ln_b[j][None]
            wm_p, bm_p, _ = _mix_operands(gmlp_w_s[j], gmlp_b_s[j], sp, gd)
            wm_s, bm_s, n_mix = _mix_operands(gmlp_w_s[j], gmlp_b_s[j], ts, gd)
            xp, _ = _gmlp_layer(xp, pp, gpre, gpost, w_in, ln_g, ln_b, wm_p, bm_p, w_out, w_gate, w_proj,
                                layer=i, tokens=GMLP_TOKENS, span=CHUNK, causal=True, emit_v=False)
            assert n_mix <= CHUNK and n_s % GMLP_BLOCK == 0
            xs, v_s = _gmlp_layer(xs, ps, gpre, gpost, w_in, ln_g, ln_b, wm_s, bm_s, w_out, w_gate, w_proj,
                                  layer=i, tokens=GMLP_BLOCK, span=n_mix, causal=False, emit_v=True)
            gmlp_v_s.append(v_s.reshape(bs, ts, e))
        else:
            w = fox_w_in[j][:, :4 * e].astype(BF16)
            wf = fox_w_in[j][:, 4 * e:].astype(BF16)
            bf = fox_b_f[j][None]
            w_out = fox_w_out[j].astype(BF16)

            q, kb, vt, *kv_p, sz, lf = _fox_proj(xp, gpre, w, wf, bf, kv_p, layer=j, groups=bp, rows=sp,
                                                 heads=nh, tokens=PROJ_TOKENS, folded_bias=True)
            o = _fox_attn(q, kb, vt)
            xp = _fox_post(o.reshape(n_p, e), sz, xp, pp, gpost, w_out, w_gate, w_proj, layer=i,
                           tokens=POST_TOKENS)
            flf_p.append(lf.reshape(bp, sp, nh))

            q, kb, vb, *kv_s, sz, lf = _fox_proj(xs, gpre, w, wf, bf, kv_s, layer=j, groups=1, rows=n_s,
                                                 heads=nh, tokens=n_s, folded_bias=False)
            lf3 = lf.reshape(bs, ts, nh)
            lf3_t = jnp.pad(lf3.transpose(0, 2, 1), ((0, 0), (0, 0), (0, LANES - ts)))
            o = _fox_sample_attn(q, kb, vb, cache_k, cache_v, j, cache_fox_logf[j].transpose(0, 2, 1),
                                 lf3, lf3_t, batch=bs)
            xs = _fox_post(o, sz, xs, ps, gpost, w_out, w_gate, w_proj, layer=i, tokens=n_s)
            flf_s.append(lf3)
    fk_p, fv_p = (a.reshape(n_fox, bp, sp, nh, dh) for a in kv_p)
    fk_s, fv_s = (a.reshape(n_fox, bs, ts, nh, dh) for a in kv_s)
    return (xp.reshape(bp, sp, d), xs.reshape(bs, ts, d), jnp.stack(gmlp_v_s),
            fk_p, fv_p, jnp.stack(flf_p), fk_s, fv_s, jnp.stack(flf_s))
```

```python
import functools

import jax
import jax.numpy as jnp
from jax import lax
from jax.experimental import pallas as pl
from jax.experimental.pallas import tpu as pltpu

F32 = jnp.float32
BF16 = jnp.bfloat16

CHUNK = 64
GMLP_BLOCK = 128
RMS_EPS = 1e-6
LN_EPS = 1e-5
LANES = 128
VMEM_LIMIT_BYTES = 56 * 1024 * 1024

GMLP_TOKENS = 256
PROJ_TOKENS = 256
POST_TOKENS = 512
ATTN_QUERIES = 512
ATTN_KEYS = 512
ATTN_CAUSAL_BLOCK = 256
ATTN_HEADS = 4
CACHE_CHUNK = 256
LOG2E = 1.4426950408889634
CUMSUM_CHUNK = 256

_NT = (((1,), (1,)), ((), ()))


def _params(n_axes):
    return pltpu.CompilerParams(
        dimension_semantics=("arbitrary",) * n_axes, vmem_limit_bytes=VMEM_LIMIT_BYTES)


def _resident(shape):
    zeros = (0,) * len(shape)
    return pl.BlockSpec(shape, lambda *_: zeros, pipeline_mode=pl.Buffered(1))


def _dot(a, b):
    return jnp.dot(a, b, preferred_element_type=F32)


def _rms_norm(x, g):
    return x * lax.rsqrt(jnp.mean(x * x, axis=-1, keepdims=True) + RMS_EPS) * g


def _silu(z):
    return z * jax.nn.sigmoid(z)


def _split3(x):
    a = x.astype(BF16)
    r = x - a.astype(F32)
    b = r.astype(BF16)
    c = (r - b.astype(F32)).astype(BF16)
    return a, b, c


def _dot_ones_lhs(ones, x):
    a, b, c = _split3(x)
    return (_dot(ones, a) + _dot(ones, b)) + _dot(ones, c)


def _dot_ones_rhs(x, ones):
    a, b, c = _split3(x)
    return (_dot(a, ones) + _dot(b, ones)) + _dot(c, ones)


def _tri(n, upper):
    r = lax.broadcasted_iota(jnp.int32, (n, n), 0)
    c = lax.broadcasted_iota(jnp.int32, (n, n), 1)
    keep = (r <= c) if upper else (c <= r)
    return jnp.where(keep, 1.0, 0.0).astype(BF16)


def _residual_and_embedding(x, o, p, gpost_ref, wgate_ref, wproj_ref):
    x1 = x + _rms_norm(o, gpost_ref[...])
    gate = jax.nn.sigmoid(_dot(x1.astype(BF16), wgate_ref[...]))
    return x1 + gate * _dot(p.astype(BF16), wproj_ref[...])


def _gmlp_layer_kernel(x_ref, p_ref, gpre_ref, gpost_ref, win_ref, lng_ref, lnb_ref,
                       wmix_ref, bmix_ref, wout_ref, wgate_ref, wproj_ref, *rest,
                       span_shift, causal, emit_v):
    out_ref, *extra, y_ref = rest
    v_ref = extra.pop(0) if emit_v else None
    for zero_ref in extra:
        zero_ref[...] = jnp.zeros_like(zero_ref)
    tm = x_ref.shape[0]
    e = wout_ref.shape[0]
    groups, blk, _ = wmix_ref.shape
    gd = e // groups

    x = x_ref[...]
    h = _rms_norm(x, gpre_ref[...]).astype(BF16)
    u = jax.nn.gelu(_dot(h, win_ref[:, 0:e]))
    gv = jax.nn.gelu(_dot(h, win_ref[:, e:2 * e]))
    sz = _silu(_dot(h, win_ref[:, 2 * e:3 * e]))
    d = gv - jnp.mean(gv, axis=-1, keepdims=True)
    v = d * lax.rsqrt(jnp.mean(d * d, axis=-1, keepdims=True) + LN_EPS) * lng_ref[...] + lnb_ref[...]
    if emit_v:
        v_ref[...] = v
    vb = v.astype(BF16)

    qs = lax.shift_right_logical(lax.broadcasted_iota(jnp.int32, (blk, blk), 0), span_shift)
    ks = lax.shift_right_logical(lax.broadcasted_iota(jnp.int32, (blk, blk), 1), span_shift)
    allowed = (ks <= qs) if causal else (ks == qs)
    row_blocks = [slice(r * blk, (r + 1) * blk) for r in range(tm // blk)]
    for g in range(groups):
        cs = slice(g * gd, (g + 1) * gd)
        wm = jnp.where(allowed, wmix_ref[g], 0.0).astype(BF16)
        mixed = _dot(wm, jnp.concatenate([vb[rs, cs] for rs in row_blocks], axis=1))
        for r, rs in enumerate(row_blocks):
            s = mixed[:, r * gd:(r + 1) * gd] + bmix_ref[:, cs]
            y_ref[rs, cs] = (u[rs, cs] * s * sz[rs, cs]).astype(BF16)
    o = _dot(y_ref[...], wout_ref[...])
    out_ref[...] = _residual_and_embedding(x, o, p_ref[...], gpost_ref, wgate_ref, wproj_ref)


def _gmlp_layer(x, p, gpre, gpost, w_in, ln_g, ln_b, w_mix, b_mix, w_out, w_gate, w_proj,
                *, layer, tokens, span, causal, emit_v, zero_buffers=()):
    n, d = x.shape
    e = w_out.shape[0]
    pd = p.shape[2]
    assert n % tokens == 0 and tokens % GMLP_BLOCK == 0 and span & (span - 1) == 0
    row = lambda i: (i, 0)
    out_shape = [jax.ShapeDtypeStruct((n, d), F32)]
    out_specs = [pl.BlockSpec((tokens, d), row)]
    if emit_v:
        out_shape.append(jax.ShapeDtypeStruct((n, e), F32))
        out_specs.append(pl.BlockSpec((tokens, e), row))
    steps = n // tokens
    for rows, lanes in zero_buffers:
        assert rows % steps == 0
        out_shape.append(jax.ShapeDtypeStruct((rows, lanes), F32))
        out_specs.append(pl.BlockSpec((rows // steps, lanes), row))
    res = pl.pallas_call(
        functools.partial(_gmlp_layer_kernel, span_shift=span.bit_length() - 1,
                          causal=causal, emit_v=emit_v),
        grid=(steps,),
        in_specs=[
            pl.BlockSpec((tokens, d), row),
            pl.BlockSpec((None, tokens, pd), lambda i: (layer, i, 0)),
            _resident(gpre.shape), _resident(gpost.shape), _resident(w_in.shape),
            _resident(ln_g.shape), _resident(ln_b.shape), _resident(w_mix.shape),
            _resident(b_mix.shape), _resident(w_out.shape), _resident(w_gate.shape),
            _resident(w_proj.shape),
        ],
        out_specs=out_specs,
        out_shape=out_shape,
        scratch_shapes=[pltpu.VMEM((tokens, e), BF16)],
        compiler_params=_params(1),
        name="gmlp_layer",
    )(x, p, gpre, gpost, w_in, ln_g, ln_b, w_mix, b_mix, w_out, w_gate, w_proj)
    return (res[0], res[1] if emit_v else None, *res[2 if emit_v else 1:])


def _bias_columns(c, hd, width):
    hi, mid, lo = _split3(c[:, hd:hd + 1])
    shape = (c.shape[0], width)
    return tuple(jnp.broadcast_to(t.astype(F32), shape) for t in (hi, mid, lo))


def _fox_proj_kernel(x_ref, gpre_ref, w_ref, wf_ref, bf_ref, kf_in_ref, vf_in_ref,
                     q_ref, k_ref, v_ref, kf_ref, vf_ref, sz_ref, logf_ref, *scratch, scale, folded_bias):
    del kf_in_ref, vf_in_ref
    tm, e = sz_ref.shape
    nh = q_ref.shape[1]
    dh = e // nh
    heads = [(hd, slice(hd * dh, (hd + 1) * dh)) for hd in range(nh)]
    h = _rms_norm(x_ref[...], gpre_ref[...]).astype(BF16)
    logf = jax.nn.log_sigmoid(_dot(h, wf_ref[...]) + bf_ref[...])
    logf_ref[...] = logf

    if folded_bias:
        (carry_ref,) = scratch

        @pl.when(pl.program_id(1) == 0)
        def _():
            carry_ref[...] = jnp.zeros_like(carry_ref)

        c = _dot_ones_lhs(_tri(tm, upper=False), logf) + carry_ref[...]
        carry_ref[...] = c[tm - 1:tm, :]
        c = c * LOG2E
        lane = lax.broadcasted_iota(jnp.int32, (tm, dh), 1)
        third = lane - jnp.where(lane >= 3, 3, 0)
        in_q, in_k = lane < 3, (lane >= 3) & (lane < 6)
        ones_q = jnp.where(in_k, 1.0, 0.0)
        ones_k = jnp.where(in_q, 1.0, 0.0)
        for hd, _ in heads:
            hi, mid, lo = _bias_columns(c, hd, dh)
            terms = jnp.where(third == 0, hi, jnp.where(third == 1, mid, lo))
            q_ref[0, hd, dh:2 * dh, :] = jnp.where(in_q, terms, ones_q).T.astype(BF16)
            k_ref[0, hd, :, dh:2 * dh] = jnp.where(in_k, -terms, ones_k).astype(BF16)

    q = _dot(h, w_ref[:, 0:e]) * scale
    for hd, hs in heads:
        if folded_bias:
            q_ref[0, hd, 0:dh, :] = q[:, hs].T.astype(BF16)
        else:
            q_ref[0, hd] = q[:, hs].astype(BF16)
    k = _dot(h, w_ref[:, e:2 * e])
    for hd, hs in heads:
        kf_ref.at[0, 0][pl.ds(hd, tm, stride=nh), :] = k[:, hs]
        if folded_bias:
            k_ref[0, hd, :, 0:dh] = k[:, hs].astype(BF16)
        else:
            k_ref[0, hd] = k[:, hs].astype(BF16)
    v = _dot(h, w_ref[:, 2 * e:3 * e])
    for hd, hs in heads:
        vf_ref.at[0, 0][pl.ds(hd, tm, stride=nh), :] = v[:, hs]
        v_ref[0, hd] = (v[:, hs].T if folded_bias else v[:, hs]).astype(BF16)
    sz_ref[...] = _silu(_dot(h, w_ref[:, 3 * e:4 * e])).astype(BF16)


def _fox_proj(x, gpre, w, wf, bf, kv_leaves, *, layer, groups, rows, heads, tokens, folded_bias):
    n, d = x.shape
    e = (w.shape[1]) // 4
    dh = e // heads
    assert n == groups * rows and rows % tokens == 0
    steps = rows // tokens
    tok = lambda b, j: (b * steps + j, 0)
    if folded_bias:
        q_shape = jax.ShapeDtypeStruct((groups, heads, 2 * dh, rows), BF16)
        q_spec = pl.BlockSpec((1, heads, 2 * dh, tokens), lambda b, j: (b, 0, 0, j))
        k_shape = jax.ShapeDtypeStruct((groups, heads, rows, 2 * dh), BF16)
        k_spec = pl.BlockSpec((1, heads, tokens, 2 * dh), lambda b, j: (b, 0, j, 0))
        v_shape = jax.ShapeDtypeStruct((groups, heads, dh, rows), BF16)
        v_spec = pl.BlockSpec((1, heads, dh, tokens), lambda b, j: (b, 0, 0, j))
        scratch = [pltpu.VMEM((1, heads), F32)]
        scale = float(dh) ** -0.5 * LOG2E
    else:
        q_shape = k_shape = v_shape = jax.ShapeDtypeStruct((groups, heads, rows, dh), BF16)
        q_spec = k_spec = v_spec = pl.BlockSpec((1, heads, tokens, dh), lambda b, j: (b, 0, j, 0))
        scratch = []
        scale = float(dh) ** -0.5
    k_leaf, v_leaf = kv_leaves
    assert k_leaf.shape == v_leaf.shape == (k_leaf.shape[0], groups, rows * heads, dh)
    leaf_shape = jax.ShapeDtypeStruct(k_leaf.shape, F32)
    leaf_spec = pl.BlockSpec((1, 1, tokens * heads, dh), lambda b, j: (layer, b, j, 0))
    return pl.pallas_call(
        functools.partial(_fox_proj_kernel, scale=scale, folded_bias=folded_bias),
        grid=(groups, steps),
        in_specs=[pl.BlockSpec((tokens, d), tok), _resident(gpre.shape), _resident(w.shape),
                  _resident(wf.shape), _resident(bf.shape),
                  pl.BlockSpec(memory_space=pl.ANY), pl.BlockSpec(memory_space=pl.ANY)],
        out_specs=[q_spec, k_spec, v_spec, leaf_spec, leaf_spec,
                   pl.BlockSpec((tokens, e), tok), pl.BlockSpec((tokens, heads), tok)],
        out_shape=[q_shape, k_shape, v_shape, leaf_shape, leaf_shape,
                   jax.ShapeDtypeStruct((n, e), BF16), jax.ShapeDtypeStruct((n, heads), F32)],
        input_output_aliases={5: 3, 6: 4},
        scratch_shapes=scratch,
        compiler_params=_params(2),
        name="fox_proj",
    )(x, gpre, w, wf, bf, k_leaf, v_leaf)


def _fox_attn_kernel(qt_ref, k_ref, vt_ref, o_ref, s_ref, smax_ref, m_ref, l_ref, acc_ref):
    hpb, tq = qt_ref.shape[1], qt_ref.shape[3]
    tk = s_ref.shape[2]
    qi = pl.program_id(2)

    def produce(j, slot, hh):
        k0 = pl.multiple_of(j * tk, tk)
        s = _dot(k_ref[0, hh, pl.ds(k0, tk), :], qt_ref[0, hh])
        s_ref[slot, hh] = s
        smax_ref[slot, hh] = jnp.max(s, axis=0, keepdims=True)

    def consume_full(j, slot, hh):
        k0 = pl.multiple_of(j * tk, tk)
        m = m_ref[hh]
        m_new = jnp.maximum(m, smax_ref[slot, hh])
        alpha = jnp.exp2(m - m_new)
        p = jnp.exp2(s_ref[slot, hh] - m_new)
        m_ref[hh] = m_new
        l_ref[hh] = alpha * l_ref[hh] + jnp.sum(p, axis=0, keepdims=True)
        acc_ref[hh] = alpha * acc_ref[hh] + _dot(vt_ref[0, hh, :, pl.ds(k0, tk)], p.astype(BF16))

    def consume_diagonal(j, slot, hh):
        k0 = pl.multiple_of(j * tk, tk)
        sub = ATTN_CAUSAL_BLOCK
        on_or_below = (lax.broadcasted_iota(jnp.int32, (sub, sub), 0)
                       <= lax.broadcasted_iota(jnp.int32, (sub, sub), 1))
        for b in range(tq // sub):
            cols = slice(b * sub, (b + 1) * sub)
            edge = jnp.where(on_or_below, s_ref[slot, hh, b * sub:(b + 1) * sub, cols], -jnp.inf)
            s_max = jnp.max(edge, axis=0, keepdims=True)
            if b:
                whole = s_ref[slot, hh, 0:b * sub, cols]
                s_max = jnp.maximum(s_max, jnp.max(whole, axis=0, keepdims=True))
            m = m_ref[hh, :, cols]
            m_new = jnp.maximum(m, s_max)
            alpha = jnp.exp2(m - m_new)
            p = jnp.exp2(edge - m_new)
            l_new = jnp.sum(p, axis=0, keepdims=True)
            pv = _dot(vt_ref[0, hh, :, pl.ds(pl.multiple_of(k0 + b * sub, sub), sub)], p.astype(BF16))
            if b:
                p = jnp.exp2(whole - m_new)
                l_new = l_new + jnp.sum(p, axis=0, keepdims=True)
                pv = pv + _dot(vt_ref[0, hh, :, pl.ds(k0, b * sub)], p.astype(BF16))
            m_ref[hh, :, cols] = m_new
            l_ref[hh, :, cols] = alpha * l_ref[hh, :, cols] + l_new
            acc_ref[hh, :, cols] = alpha * acc_ref[hh, :, cols] + pv

    def consume(j, slot, hh, diagonal):
        (consume_diagonal if diagonal else consume_full)(j, slot, hh)

    def step(produced, consumed):
        for hh in range(hpb):
            if produced is not None:
                produce(*produced, hh)
            if consumed is not None:
                consume(*consumed[:2], hh, consumed[2])

    m_ref[...] = jnp.full_like(m_ref, -jnp.inf)
    l_ref[...] = jnp.zeros_like(l_ref)
    acc_ref[...] = jnp.zeros_like(acc_ref)
    n_full = (qi * tq) // tk
    step((0, 0), None)

    @pl.loop(0, n_full // 2)
    def _(i):
        step((2 * i + 1, 1), (2 * i, 0, False))
        step((2 * i + 2, 0), (2 * i + 1, 1, False))

    @pl.when(n_full % 2 == 1)
    def _():
        step((n_full, 1), (n_full - 1, 0, False))
        step(None, (n_full, 1, True))

    @pl.when(n_full % 2 == 0)
    def _():
        step(None, (n_full, 0, True))

    dh = acc_ref.shape[1]
    for hh in range(hpb):
        o_ref[0, :, hh * dh:(hh + 1) * dh] = (acc_ref[hh] / l_ref[hh]).T.astype(BF16)


def _fox_attn(qt, k, vt):
    b, nh, s, dh2 = k.shape
    dh = vt.shape[2]
    t, tk, hpb = ATTN_QUERIES, ATTN_KEYS, ATTN_HEADS
    assert s % tk == 0 and tk == t and t % ATTN_CAUSAL_BLOCK == 0
    assert nh % hpb == 0 and dh == LANES and dh2 == 2 * dh
    return pl.pallas_call(
        _fox_attn_kernel,
        grid=(b, nh // hpb, s // t),
        in_specs=[
            pl.BlockSpec((1, hpb, dh2, t), lambda bi, hi, qi: (bi, hi, 0, qi)),
            pl.BlockSpec((1, hpb, s, dh2), lambda bi, hi, qi: (bi, hi, 0, 0)),
            pl.BlockSpec((1, hpb, dh, s), lambda bi, hi, qi: (bi, hi, 0, 0)),
        ],
        out_specs=pl.BlockSpec((1, t, hpb * dh), lambda bi, hi, qi: (bi, qi, hi)),
        out_shape=jax.ShapeDtypeStruct((b, s, nh * dh), BF16),
        scratch_shapes=[pltpu.VMEM((2, hpb, tk, t), F32), pltpu.VMEM((2, hpb, 1, t), F32),
                        pltpu.VMEM((hpb, 1, t), F32), pltpu.VMEM((hpb, 1, t), F32),
                        pltpu.VMEM((hpb, dh, t), F32)],
        compiler_params=_params(3),
        name="fox_attn",
    )(qt, k, vt)


def _fox_sample_attn_kernel(q_ref, kc_ref, vc_ref, kn_ref, vn_ref, lfct_ref, lfn_ref, lfnt_ref,
                            o_ref, cct_ref, cnt_ref, cn_ref, m_ref, l_ref, acc_ref, s_ref, sn_ref, pv_ref):
    nh, t, dh = q_ref.shape[1], q_ref.shape[2], q_ref.shape[3]
    past = lfct_ref.shape[2]
    pad = lfnt_ref.shape[2]
    pc = kc_ref.shape[2] // nh
    step = pl.program_id(1)

    @pl.when(step == 0)
    def _():
        triu = _tri(CUMSUM_CHUNK, upper=True)
        total = jnp.zeros((nh, 1), F32)
        for i in range(past // CUMSUM_CHUNK):
            cs = slice(i * CUMSUM_CHUNK, (i + 1) * CUMSUM_CHUNK)
            blk = _dot_ones_rhs(lfct_ref[0, :, cs], triu) + total
            cct_ref[:, cs] = blk
            total = blk[:, CUMSUM_CHUNK - 1:CUMSUM_CHUNK]
        eye = (lax.broadcasted_iota(jnp.int32, (nh, nh), 0)
               == lax.broadcasted_iota(jnp.int32, (nh, nh), 1))
        total_row = jnp.sum(jnp.where(eye, total, 0.0), axis=0, keepdims=True)
        cn_ref[...] = _dot_ones_lhs(_tri(t, upper=False), lfn_ref[0]) + total_row
        cnt_ref[...] = _dot_ones_rhs(lfnt_ref[0], _tri(pad, upper=True)) + total
        m_ref[...] = jnp.full_like(m_ref, -jnp.inf)
        l_ref[...] = jnp.zeros_like(l_ref)
        acc_ref[...] = jnp.zeros_like(acc_ref)

    def rows(hd):
        return slice(hd * t, (hd + 1) * t)

    def attend(s, values):
        m = m_ref[...]
        m_new = jnp.maximum(m, jnp.max(s, axis=1, keepdims=True))
        alpha = jnp.exp(m - m_new)
        p = jnp.exp(s - m_new)
        m_ref[...] = m_new
        l_ref[...] = alpha * l_ref[...] + jnp.sum(p, axis=1, keepdims=True)
        p = p.astype(BF16)
        for hd in range(nh):
            pv_ref[rows(hd), :] = _dot(p[rows(hd), :], values(hd))
        acc_ref[...] = alpha * acc_ref[...] + pv_ref[...]

    p0 = pl.multiple_of(step * pc, pc)
    for hd in range(nh):
        kc = kc_ref.at[0, 0][pl.ds(hd, pc, stride=nh), :].astype(BF16)
        s_ref[rows(hd), :] = (lax.dot_general(q_ref[0, hd], kc, _NT, preferred_element_type=F32)
                              + cn_ref[:, hd:hd + 1] - cct_ref[hd:hd + 1, pl.ds(p0, pc)])
    attend(s_ref[...], lambda hd: vc_ref.at[0, 0][pl.ds(hd, pc, stride=nh), :].astype(BF16))

    @pl.when(step == pl.num_programs(1) - 1)
    def _():
        visible = (lax.broadcasted_iota(jnp.int32, (t, pad), 1)
                   <= lax.broadcasted_iota(jnp.int32, (t, pad), 0))
        zeros = jnp.zeros((pad - t, dh), BF16)
        for hd in range(nh):
            kn = jnp.concatenate([kn_ref[0, hd], zeros], axis=0)
            s = (lax.dot_general(q_ref[0, hd], kn, _NT, preferred_element_type=F32)
                 + cn_ref[:, hd:hd + 1] - cnt_ref[hd:hd + 1, :])
            sn_ref[rows(hd), :] = jnp.where(visible, s, -jnp.inf)
        attend(sn_ref[...], lambda hd: jnp.concatenate([vn_ref[0, hd], zeros], axis=0))
        o = (acc_ref[...] / l_ref[...]).astype(BF16)
        for hd in range(nh):
            o_ref[:, hd * dh:(hd + 1) * dh] = o[rows(hd), :]


def _fox_sample_attn(q, kn, vn, k_cache, v_cache, layer, logf_cache_t, logf_new, logf_new_t, *, batch):
    _, nh, n, dh = q.shape
    t = n // batch
    past = k_cache.shape[2] // nh
    pc = CACHE_CHUNK
    assert past % pc == 0 and past % CUMSUM_CHUNK == 0 and dh == LANES
    new_spec = pl.BlockSpec((1, nh, t, dh), lambda bi, si: (0, 0, bi, 0))
    cache_spec = pl.BlockSpec((1, 1, pc * nh, dh), lambda bi, si: (layer, bi, si, 0))
    whole = lambda shape: pl.BlockSpec((1,) + shape[1:], lambda bi, si: (bi, 0, 0))
    pad = logf_new_t.shape[2]
    return pl.pallas_call(
        _fox_sample_attn_kernel,
        grid=(batch, past // pc),
        in_specs=[new_spec, cache_spec, cache_spec, new_spec, new_spec,
                  whole(logf_cache_t.shape), whole(logf_new.shape), whole(logf_new_t.shape)],
        out_specs=pl.BlockSpec((t, nh * dh), lambda bi, si: (bi, 0)),
        out_shape=jax.ShapeDtypeStruct((n, nh * dh), BF16),
        scratch_shapes=[pltpu.VMEM((nh, past), F32), pltpu.VMEM((nh, pad), F32),
                        pltpu.VMEM((t, nh), F32), pltpu.VMEM((nh * t, 1), F32),
                        pltpu.VMEM((nh * t, 1), F32), pltpu.VMEM((nh * t, dh), F32),
                        pltpu.VMEM((nh * t, pc), F32), pltpu.VMEM((nh * t, pad), F32),
                        pltpu.VMEM((nh * t, dh), F32)],
        compiler_params=_params(2),
        name="fox_sample_attn",
    )(q, k_cache, v_cache, kn, vn, logf_cache_t, logf_new, logf_new_t)


def _fox_post_kernel(o_ref, sz_ref, x_ref, p_ref, gpost_ref, wout_ref, wgate_ref, wproj_ref, out_ref):
    o = _dot(o_ref[...] * sz_ref[...], wout_ref[...])
    out_ref[...] = _residual_and_embedding(x_ref[...], o, p_ref[...], gpost_ref, wgate_ref, wproj_ref)


def _fox_post(o, sz, x, p, gpost, w_out, w_gate, w_proj, *, layer, tokens):
    n, d = x.shape
    e = o.shape[1]
    assert n % tokens == 0
    row = lambda i: (i, 0)
    return pl.pallas_call(
        _fox_post_kernel,
        grid=(n // tokens,),
        in_specs=[pl.BlockSpec((tokens, e), row), pl.BlockSpec((tokens, e), row),
                  pl.BlockSpec((tokens, d), row),
                  pl.BlockSpec((None, tokens, p.shape[2]), lambda i: (layer, i, 0)),
                  _resident(gpost.shape), _resident(w_out.shape), _resident(w_gate.shape),
                  _resident(w_proj.shape)],
        out_specs=pl.BlockSpec((tokens, d), row),
        out_shape=jax.ShapeDtypeStruct((n, d), F32),
        compiler_params=_params(1),
        name="fox_post",
    )(o, sz, x, p, gpost, w_out, w_gate, w_proj)


def _mix_operands(w_s, b_s, rows, gd):
    n = min(rows, GMLP_BLOCK)
    reps = GMLP_BLOCK // n
    w = jnp.tile(w_s[:, :n, :n], (1, reps, reps))
    b = jnp.repeat(jnp.tile(b_s[:, :n], (1, reps)).T, gd, axis=1)
    return w, b, n


def kernel(x_prompt, x_sample, cache_fox_k, cache_fox_v, cache_fox_logf, p_prompt, p_sample, norm_pre, norm_post, gmlp_w_in, gmlp_ln_g, gmlp_ln_b, gmlp_w_s, gmlp_b_s, gmlp_w_out, fox_w_in, fox_b_f, fox_w_out, ple_w_proj, ple_w_gate):
    bp, sp, d = x_prompt.shape
    bs, ts, _ = x_sample.shape
    depth = norm_pre.shape[0]
    n_fox, _, past, nh, dh = cache_fox_k.shape
    e = nh * dh
    groups = gmlp_w_s.shape[1]
    gd = e // groups
    n_p, n_s = bp * sp, bs * ts

    xp = x_prompt.reshape(n_p, d)
    xs = x_sample.reshape(n_s, d)
    cache_k = cache_fox_k.reshape(cache_fox_k.shape[0], bs, past * nh, dh)
    cache_v = cache_fox_v.reshape(cache_fox_v.shape[0], bs, past * nh, dh)
    pp = p_prompt.reshape(depth, n_p, -1)
    ps = p_sample.reshape(depth, n_s, -1)
    gmlp_v_s = []
    flf_p, flf_s = [], []
    kv_p = None
    kv_s = (jnp.zeros((n_fox, 1, n_s * nh, dh), F32),) * 2
    for i in range(depth):
        j = i // 2
        gpre, gpost = norm_pre[i][None], norm_post[i][None]
        w_gate, w_proj = ple_w_gate[i].astype(BF16), ple_w_proj[i].astype(BF16)
        if i % 2 == 0:
            w_in, w_out = gmlp_w_in[j].astype(BF16), gmlp_w_out[j].astype(BF16)
            ln_g, ln_b = gmlp_ln_g[j][None], gmlp_ln_b[j][None]
            wm_p, bm_p, _ = _mix_operands(gmlp_w_s[j], gmlp_b_s[j], sp, gd)
            wm_s, bm_s, n_mix = _mix_operands(gmlp_w_s[j], gmlp_b_s[j], ts, gd)
            leaves = [(n_fox * n_p * nh, dh)] * 2 if kv_p is None else []
            xp, _, *zeroed = _gmlp_layer(xp, pp, gpre, gpost, w_in, ln_g, ln_b, wm_p, bm_p, w_out, w_gate,
                                         w_proj, layer=i, tokens=GMLP_TOKENS, span=CHUNK, causal=True,
                                         emit_v=False, zero_buffers=leaves)
            if zeroed:
                kv_p = [z.reshape(n_fox, bp, sp * nh, dh) for z in zeroed]
            assert n_mix <= CHUNK and n_s % GMLP_BLOCK == 0
            xs, v_s = _gmlp_layer(xs, ps, gpre, gpost, w_in, ln_g, ln_b, wm_s, bm_s, w_out, w_gate, w_proj,
                                  layer=i, tokens=GMLP_BLOCK, span=n_mix, causal=False, emit_v=True)
            gmlp_v_s.append(v_s.reshape(bs, ts, e))
        else:
            w = fox_w_in[j][:, :4 * e].astype(BF16)
            wf = fox_w_in[j][:, 4 * e:].astype(BF16)
            bf = fox_b_f[j][None]
            w_out = fox_w_out[j].astype(BF16)

            q, kb, vt, *kv_p, sz, lf = _fox_proj(xp, gpre, w, wf, bf, kv_p, layer=j, groups=bp, rows=sp,
                                                 heads=nh, tokens=PROJ_TOKENS, folded_bias=True)
            o = _fox_attn(q, kb, vt)
            xp = _fox_post(o.reshape(n_p, e), sz, xp, pp, gpost, w_out, w_gate, w_proj, layer=i,
                           tokens=POST_TOKENS)
            flf_p.append(lf.reshape(bp, sp, nh))

            q, kb, vb, *kv_s, sz, lf = _fox_proj(xs, gpre, w, wf, bf, kv_s, layer=j, groups=1, rows=n_s,
                                                 heads=nh, tokens=n_s, folded_bias=False)
            lf3 = lf.reshape(bs, ts, nh)
            lf3_t = jnp.pad(lf3.transpose(0, 2, 1), ((0, 0), (0, 0), (0, LANES - ts)))
            o = _fox_sample_attn(q, kb, vb, cache_k, cache_v, j, cache_fox_logf[j].transpose(0, 2, 1),
                                 lf3, lf3_t, batch=bs)
            xs = _fox_post(o, sz, xs, ps, gpost, w_out, w_gate, w_proj, layer=i, tokens=n_s)
            flf_s.append(lf3)
    fk_p, fv_p = (a.reshape(n_fox, bp, sp, nh, dh) for a in kv_p)
    fk_s, fv_s = (a.reshape(n_fox, bs, ts, nh, dh) for a in kv_s)
    return (xp.reshape(bp, sp, d), xs.reshape(bs, ts, d), jnp.stack(gmlp_v_s),
            fk_p, fv_p, jnp.stack(flf_p), fk_s, fv_s, jnp.stack(flf_s))
```

```python
import functools

import jax
import jax.numpy as jnp
from jax import lax
from jax.experimental import pallas as pl
from jax.experimental.pallas import tpu as pltpu

F32 = jnp.float32
BF16 = jnp.bfloat16

CHUNK = 64
GMLP_BLOCK = 128
RMS_EPS = 1e-6
LN_EPS = 1e-5
LANES = 128
VMEM_LIMIT_BYTES = 56 * 1024 * 1024

GMLP_TOKENS = 256
PROJ_TOKENS = 256
POST_TOKENS = 512
ATTN_QUERIES = 512
ATTN_KEYS = 512
ATTN_CAUSAL_BLOCK = 256
ATTN_HEADS = 4
CACHE_CHUNK = 256
LOG2E = 1.4426950408889634
CUMSUM_CHUNK = 256

_NT = (((1,), (1,)), ((), ()))


def _params(n_axes):
    return pltpu.CompilerParams(
        dimension_semantics=("arbitrary",) * n_axes, vmem_limit_bytes=VMEM_LIMIT_BYTES)


def _resident(shape):
    zeros = (0,) * len(shape)
    return pl.BlockSpec(shape, lambda *_: zeros, pipeline_mode=pl.Buffered(1))


def _dot(a, b):
    return jnp.dot(a, b, preferred_element_type=F32)


def _rms_norm(x, g):
    return x * lax.rsqrt(jnp.mean(x * x, axis=-1, keepdims=True) + RMS_EPS) * g


def _silu(z):
    return z * jax.nn.sigmoid(z)


def _split3(x):
    a = x.astype(BF16)
    r = x - a.astype(F32)
    b = r.astype(BF16)
    c = (r - b.astype(F32)).astype(BF16)
    return a, b, c


def _dot_ones_lhs(ones, x):
    a, b, c = _split3(x)
    return (_dot(ones, a) + _dot(ones, b)) + _dot(ones, c)


def _dot_ones_rhs(x, ones):
    a, b, c = _split3(x)
    return (_dot(a, ones) + _dot(b, ones)) + _dot(c, ones)


def _tri(n, upper):
    r = lax.broadcasted_iota(jnp.int32, (n, n), 0)
    c = lax.broadcasted_iota(jnp.int32, (n, n), 1)
    keep = (r <= c) if upper else (c <= r)
    return jnp.where(keep, 1.0, 0.0).astype(BF16)


def _residual_and_embedding(x, o, p, gpost_ref, wgate_ref, wproj_ref):
    x1 = x + _rms_norm(o, gpost_ref[...])
    gate = jax.nn.sigmoid(_dot(x1.astype(BF16), wgate_ref[...]))
    return x1 + gate * _dot(p.astype(BF16), wproj_ref[...])


def _gmlp_layer_kernel(x_ref, p_ref, gpre_ref, gpost_ref, win_ref, lng_ref, lnb_ref,
                       wmix_ref, bmix_ref, wout_ref, wgate_ref, wproj_ref, *rest,
                       span_shift, causal, emit_v):
    out_ref, *extra, y_ref = rest
    v_ref = extra.pop(0) if emit_v else None
    for zero_ref in extra:
        zero_ref[...] = jnp.zeros_like(zero_ref)
    tm = x_ref.shape[0]
    e = wout_ref.shape[0]
    groups, blk, _ = wmix_ref.shape
    gd = e // groups

    x = x_ref[...]
    h = _rms_norm(x, gpre_ref[...]).astype(BF16)
    u = jax.nn.gelu(_dot(h, win_ref[:, 0:e]))
    gv = jax.nn.gelu(_dot(h, win_ref[:, e:2 * e]))
    sz = _silu(_dot(h, win_ref[:, 2 * e:3 * e]))
    d = gv - jnp.mean(gv, axis=-1, keepdims=True)
    v = d * lax.rsqrt(jnp.mean(d * d, axis=-1, keepdims=True) + LN_EPS) * lng_ref[...] + lnb_ref[...]
    if emit_v:
        v_ref[...] = v
    vb = v.astype(BF16)

    qs = lax.shift_right_logical(lax.broadcasted_iota(jnp.int32, (blk, blk), 0), span_shift)
    ks = lax.shift_right_logical(lax.broadcasted_iota(jnp.int32, (blk, blk), 1), span_shift)
    allowed = (ks <= qs) if causal else (ks == qs)
    row_blocks = [slice(r * blk, (r + 1) * blk) for r in range(tm // blk)]
    for g in range(groups):
        cs = slice(g * gd, (g + 1) * gd)
        wm = jnp.where(allowed, wmix_ref[g], 0.0).astype(BF16)
        mixed = _dot(wm, jnp.concatenate([vb[rs, cs] for rs in row_blocks], axis=1))
        for r, rs in enumerate(row_blocks):
            s = mixed[:, r * gd:(r + 1) * gd] + bmix_ref[:, cs]
            y_ref[rs, cs] = (u[rs, cs] * s * sz[rs, cs]).astype(BF16)
    o = _dot(y_ref[...], wout_ref[...])
    out_ref[...] = _residual_and_embedding(x, o, p_ref[...], gpost_ref, wgate_ref, wproj_ref)


def _gmlp_layer(x, p, gpre, gpost, w_in, ln_g, ln_b, w_mix, b_mix, w_out, w_gate, w_proj,
                *, layer, tokens, span, causal, emit_v, zero_buffers=()):
    n, d = x.shape
    e = w_out.shape[0]
    pd = p.shape[2]
    assert n % tokens == 0 and tokens % GMLP_BLOCK == 0 and span & (span - 1) == 0
    row = lambda i: (i, 0)
    out_shape = [jax.ShapeDtypeStruct((n, d), F32)]
    out_specs = [pl.BlockSpec((tokens, d), row)]
    if emit_v:
        out_shape.append(jax.ShapeDtypeStruct((n, e), F32))
        out_specs.append(pl.BlockSpec((tokens, e), row))
    steps = n // tokens
    for rows, lanes in zero_buffers:
        assert rows % steps == 0
        out_shape.append(jax.ShapeDtypeStruct((rows, lanes), F32))
        out_specs.append(pl.BlockSpec((rows // steps, lanes), row))
    res = pl.pallas_call(
        functools.partial(_gmlp_layer_kernel, span_shift=span.bit_length() - 1,
                          causal=causal, emit_v=emit_v),
        grid=(steps,),
        in_specs=[
            pl.BlockSpec((tokens, d), row),
            pl.BlockSpec((None, tokens, pd), lambda i: (layer, i, 0)),
            _resident(gpre.shape), _resident(gpost.shape), _resident(w_in.shape),
            _resident(ln_g.shape), _resident(ln_b.shape), _resident(w_mix.shape),
            _resident(b_mix.shape), _resident(w_out.shape), _resident(w_gate.shape),
            _resident(w_proj.shape),
        ],
        out_specs=out_specs,
        out_shape=out_shape,
        scratch_shapes=[pltpu.VMEM((tokens, e), BF16)],
        compiler_params=_params(1),
        name="gmlp_layer",
    )(x, p, gpre, gpost, w_in, ln_g, ln_b, w_mix, b_mix, w_out, w_gate, w_proj)
    return (res[0], res[1] if emit_v else None, *res[2 if emit_v else 1:])


def _bias_columns(c, hd, width):
    hi, mid, lo = _split3(c[:, hd:hd + 1])
    shape = (c.shape[0], width)
    return tuple(jnp.broadcast_to(t.astype(F32), shape) for t in (hi, mid, lo))


def _fox_proj_kernel(x_ref, gpre_ref, w_ref, wf_ref, bf_ref, kf_in_ref, vf_in_ref,
                     q_ref, k_ref, v_ref, kf_ref, vf_ref, sz_ref, logf_ref, *scratch, scale, folded_bias):
    del kf_in_ref, vf_in_ref
    tm, e = sz_ref.shape
    nh = q_ref.shape[1]
    dh = e // nh
    heads = [(hd, slice(hd * dh, (hd + 1) * dh)) for hd in range(nh)]
    h = _rms_norm(x_ref[...], gpre_ref[...]).astype(BF16)
    logf = jax.nn.log_sigmoid(_dot(h, wf_ref[...]) + bf_ref[...])
    logf_ref[...] = logf

    if folded_bias:
        (carry_ref,) = scratch

        @pl.when(pl.program_id(1) == 0)
        def _():
            carry_ref[...] = jnp.zeros_like(carry_ref)

        c = _dot_ones_lhs(_tri(tm, upper=False), logf) + carry_ref[...]
        carry_ref[...] = c[tm - 1:tm, :]
        c = c * LOG2E
        lane = lax.broadcasted_iota(jnp.int32, (tm, dh), 1)
        third = lane - jnp.where(lane >= 3, 3, 0)
        in_q, in_k = lane < 3, (lane >= 3) & (lane < 6)
        ones_q = jnp.where(in_k, 1.0, 0.0)
        ones_k = jnp.where(in_q, 1.0, 0.0)
        for hd, _ in heads:
            hi, mid, lo = _bias_columns(c, hd, dh)
            terms = jnp.where(third == 0, hi, jnp.where(third == 1, mid, lo))
            q_ref[0, hd, dh:2 * dh, :] = jnp.where(in_q, terms, ones_q).T.astype(BF16)
            k_ref[0, hd, :, dh:2 * dh] = jnp.where(in_k, -terms, ones_k).astype(BF16)

    q = _dot(h, w_ref[:, 0:e]) * scale
    for hd, hs in heads:
        if folded_bias:
            q_ref[0, hd, 0:dh, :] = q[:, hs].T.astype(BF16)
        else:
            q_ref[0, hd] = q[:, hs].astype(BF16)
    k = _dot(h, w_ref[:, e:2 * e])
    for hd, hs in heads:
        kf_ref.at[0, 0][pl.ds(hd, tm, stride=nh), :] = k[:, hs]
        if folded_bias:
            k_ref[0, hd, :, 0:dh] = k[:, hs].astype(BF16)
        else:
            k_ref[0, hd] = k[:, hs].astype(BF16)
    v = _dot(h, w_ref[:, 2 * e:3 * e])
    for hd, hs in heads:
        vf_ref.at[0, 0][pl.ds(hd, tm, stride=nh), :] = v[:, hs]
        v_ref[0, hd] = (v[:, hs].T if folded_bias else v[:, hs]).astype(BF16)
    sz_ref[...] = _silu(_dot(h, w_ref[:, 3 * e:4 * e])).astype(BF16)


def _fox_proj(x, gpre, w, wf, bf, kv_leaves, *, layer, groups, rows, heads, tokens, folded_bias):
    n, d = x.shape
    e = (w.shape[1]) // 4
    dh = e // heads
    assert n == groups * rows and rows % tokens == 0
    steps = rows // tokens
    tok = lambda b, j: (b * steps + j, 0)
    if folded_bias:
        q_shape = jax.ShapeDtypeStruct((groups, heads, 2 * dh, rows), BF16)
        q_spec = pl.BlockSpec((1, heads, 2 * dh, tokens), lambda b, j: (b, 0, 0, j))
        k_shape = jax.ShapeDtypeStruct((groups, heads, rows, 2 * dh), BF16)
        k_spec = pl.BlockSpec((1, heads, tokens, 2 * dh), lambda b, j: (b, 0, j, 0))
        v_shape = jax.ShapeDtypeStruct((groups, heads, dh, rows), BF16)
        v_spec = pl.BlockSpec((1, heads, dh, tokens), lambda b, j: (b, 0, 0, j))
        scratch = [pltpu.VMEM((1, heads), F32)]
        scale = float(dh) ** -0.5 * LOG2E
    else:
        q_shape = k_shape = v_shape = jax.ShapeDtypeStruct((groups, heads, rows, dh), BF16)
        q_spec = k_spec = v_spec = pl.BlockSpec((1, heads, tokens, dh), lambda b, j: (b, 0, j, 0))
        scratch = []
        scale = float(dh) ** -0.5
    k_leaf, v_leaf = kv_leaves
    assert k_leaf.shape == v_leaf.shape == (k_leaf.shape[0], groups, rows * heads, dh)
    leaf_shape = jax.ShapeDtypeStruct(k_leaf.shape, F32)
    leaf_spec = pl.BlockSpec((1, 1, tokens * heads, dh), lambda b, j: (layer, b, j, 0))
    return pl.pallas_call(
        functools.partial(_fox_proj_kernel, scale=scale, folded_bias=folded_bias),
        grid=(groups, steps),
        in_specs=[pl.BlockSpec((tokens, d), tok), _resident(gpre.shape), _resident(w.shape),
                  _resident(wf.shape), _resident(bf.shape),
                  pl.BlockSpec(memory_space=pl.ANY), pl.BlockSpec(memory_space=pl.ANY)],
        out_specs=[q_spec, k_spec, v_spec, leaf_spec, leaf_spec,
                   pl.BlockSpec((tokens, e), tok), pl.BlockSpec((tokens, heads), tok)],
        out_shape=[q_shape, k_shape, v_shape, leaf_shape, leaf_shape,
                   jax.ShapeDtypeStruct((n, e), BF16), jax.ShapeDtypeStruct((n, heads), F32)],
        input_output_aliases={5: 3, 6: 4},
        scratch_shapes=scratch,
        compiler_params=_params(2),
        name="fox_proj",
    )(x, gpre, w, wf, bf, k_leaf, v_leaf)


def _fox_attn_kernel(qt_ref, qt_next_ref, k_ref, vt_ref, o_ref, s_ref, smax_ref, m_ref, l_ref, acc_ref):
    hpb, tq = qt_ref.shape[1], qt_ref.shape[3]
    tk = s_ref.shape[2]
    qi = pl.program_id(2)

    def produce(j, slot, hh, queries_ref=qt_ref):
        k0 = pl.multiple_of(j * tk, tk)
        s = _dot(k_ref[0, hh, pl.ds(k0, tk), :], queries_ref[0, hh])
        s_ref[slot, hh] = s
        smax_ref[slot, hh] = jnp.max(s, axis=0, keepdims=True)

    def consume_full(j, slot, hh):
        k0 = pl.multiple_of(j * tk, tk)
        m = m_ref[hh]
        m_new = jnp.maximum(m, smax_ref[slot, hh])
        alpha = jnp.exp2(m - m_new)
        p = jnp.exp2(s_ref[slot, hh] - m_new)
        m_ref[hh] = m_new
        l_ref[hh] = alpha * l_ref[hh] + jnp.sum(p, axis=0, keepdims=True)
        acc_ref[hh] = alpha * acc_ref[hh] + _dot(vt_ref[0, hh, :, pl.ds(k0, tk)], p.astype(BF16))

    def consume_diagonal(j, slot, hh):
        k0 = pl.multiple_of(j * tk, tk)
        sub = ATTN_CAUSAL_BLOCK
        on_or_below = (lax.broadcasted_iota(jnp.int32, (sub, sub), 0)
                       <= lax.broadcasted_iota(jnp.int32, (sub, sub), 1))
        for b in range(tq // sub):
            cols = slice(b * sub, (b + 1) * sub)
            edge = jnp.where(on_or_below, s_ref[slot, hh, b * sub:(b + 1) * sub, cols], -jnp.inf)
            s_max = jnp.max(edge, axis=0, keepdims=True)
            if b:
                whole = s_ref[slot, hh, 0:b * sub, cols]
                s_max = jnp.maximum(s_max, jnp.max(whole, axis=0, keepdims=True))
            m = m_ref[hh, :, cols]
            m_new = jnp.maximum(m, s_max)
            alpha = jnp.exp2(m - m_new)
            p = jnp.exp2(edge - m_new)
            l_new = jnp.sum(p, axis=0, keepdims=True)
            pv = _dot(vt_ref[0, hh, :, pl.ds(pl.multiple_of(k0 + b * sub, sub), sub)], p.astype(BF16))
            if b:
                p = jnp.exp2(whole - m_new)
                l_new = l_new + jnp.sum(p, axis=0, keepdims=True)
                pv = pv + _dot(vt_ref[0, hh, :, pl.ds(k0, b * sub)], p.astype(BF16))
            m_ref[hh, :, cols] = m_new
            l_ref[hh, :, cols] = alpha * l_ref[hh, :, cols] + l_new
            acc_ref[hh, :, cols] = alpha * acc_ref[hh, :, cols] + pv

    def consume(j, slot, hh, diagonal):
        (consume_diagonal if diagonal else consume_full)(j, slot, hh)

    def step(produced, consumed):
        for hh in range(hpb):
            if produced is not None:
                produce(produced[0], produced[1], hh, *produced[2:])
            if consumed is not None:
                consume(*consumed[:2], hh, consumed[2])

    m_ref[...] = jnp.full_like(m_ref, -jnp.inf)
    l_ref[...] = jnp.zeros_like(l_ref)
    acc_ref[...] = jnp.zeros_like(acc_ref)
    n_full = qi
    first = 2
    hand_over = (0, first, qt_next_ref)

    @pl.when(qi == 0)
    def _():
        step((0, 0), None)
        step(hand_over, (0, 0, True))

    @pl.when(qi > 0)
    def _():
        step((1, 1), (0, first, False))

        @pl.loop(0, (n_full - 1) // 2)
        def _(i):
            step((2 * i + 2, 0), (2 * i + 1, 1, False))
            step((2 * i + 3, 1), (2 * i + 2, 0, False))

        @pl.when(n_full % 2 == 1)
        def _():
            step(hand_over, (n_full, 1, True))

        @pl.when(n_full % 2 == 0)
        def _():
            step((n_full, 0), (n_full - 1, 1, False))
            step(hand_over, (n_full, 0, True))

    dh = acc_ref.shape[1]
    for hh in range(hpb):
        o_ref[0, :, hh * dh:(hh + 1) * dh] = (acc_ref[hh] / l_ref[hh]).T.astype(BF16)


def _fox_attn(qt, k, vt):
    b, nh, s, dh2 = k.shape
    dh = vt.shape[2]
    t, tk, hpb = ATTN_QUERIES, ATTN_KEYS, ATTN_HEADS
    assert s % tk == 0 and tk == t and t % ATTN_CAUSAL_BLOCK == 0
    assert nh % hpb == 0 and dh == LANES and dh2 == 2 * dh
    last = s // t - 1
    return pl.pallas_call(
        _fox_attn_kernel,
        grid=(b, nh // hpb, s // t),
        in_specs=[
            pl.BlockSpec((1, hpb, dh2, t), lambda bi, hi, qi: (bi, hi, 0, qi)),
            pl.BlockSpec((1, hpb, dh2, t), lambda bi, hi, qi: (bi, hi, 0, jnp.minimum(qi + 1, last))),
            pl.BlockSpec((1, hpb, s, dh2), lambda bi, hi, qi: (bi, hi, 0, 0)),
            pl.BlockSpec((1, hpb, dh, s), lambda bi, hi, qi: (bi, hi, 0, 0)),
        ],
        out_specs=pl.BlockSpec((1, t, hpb * dh), lambda bi, hi, qi: (bi, qi, hi)),
        out_shape=jax.ShapeDtypeStruct((b, s, nh * dh), BF16),
        scratch_shapes=[pltpu.VMEM((3, hpb, tk, t), F32), pltpu.VMEM((3, hpb, 1, t), F32),
                        pltpu.VMEM((hpb, 1, t), F32), pltpu.VMEM((hpb, 1, t), F32),
                        pltpu.VMEM((hpb, dh, t), F32)],
        compiler_params=_params(3),
        name="fox_attn",
    )(qt, qt, k, vt)


def _fox_sample_attn_kernel(q_ref, kc_ref, vc_ref, kn_ref, vn_ref, lfct_ref, lfn_ref, lfnt_ref,
                            o_ref, cct_ref, cnt_ref, cn_ref, m_ref, l_ref, acc_ref, s_ref, sn_ref, pv_ref):
    nh, t, dh = q_ref.shape[1], q_ref.shape[2], q_ref.shape[3]
    past = lfct_ref.shape[2]
    pad = lfnt_ref.shape[2]
    pc = kc_ref.shape[2] // nh
    step = pl.program_id(1)

    @pl.when(step == 0)
    def _():
        triu = _tri(CUMSUM_CHUNK, upper=True)
        total = jnp.zeros((nh, 1), F32)
        for i in range(past // CUMSUM_CHUNK):
            cs = slice(i * CUMSUM_CHUNK, (i + 1) * CUMSUM_CHUNK)
            blk = _dot_ones_rhs(lfct_ref[0, :, cs], triu) + total
            cct_ref[:, cs] = blk
            total = blk[:, CUMSUM_CHUNK - 1:CUMSUM_CHUNK]
        eye = (lax.broadcasted_iota(jnp.int32, (nh, nh), 0)
               == lax.broadcasted_iota(jnp.int32, (nh, nh), 1))
        total_row = jnp.sum(jnp.where(eye, total, 0.0), axis=0, keepdims=True)
        cn_ref[...] = _dot_ones_lhs(_tri(t, upper=False), lfn_ref[0]) + total_row
        cnt_ref[...] = _dot_ones_rhs(lfnt_ref[0], _tri(pad, upper=True)) + total
        m_ref[...] = jnp.full_like(m_ref, -jnp.inf)
        l_ref[...] = jnp.zeros_like(l_ref)
        acc_ref[...] = jnp.zeros_like(acc_ref)

    def rows(hd):
        return slice(hd * t, (hd + 1) * t)

    def attend(s, values):
        m = m_ref[...]
        m_new = jnp.maximum(m, jnp.max(s, axis=1, keepdims=True))
        alpha = jnp.exp(m - m_new)
        p = jnp.exp(s - m_new)
        m_ref[...] = m_new
        l_ref[...] = alpha * l_ref[...] + jnp.sum(p, axis=1, keepdims=True)
        p = p.astype(BF16)
        for hd in range(nh):
            pv_ref[rows(hd), :] = _dot(p[rows(hd), :], values(hd))
        acc_ref[...] = alpha * acc_ref[...] + pv_ref[...]

    p0 = pl.multiple_of(step * pc, pc)
    for hd in range(nh):
        kc = kc_ref.at[0, 0][pl.ds(hd, pc, stride=nh), :].astype(BF16)
        s_ref[rows(hd), :] = (lax.dot_general(q_ref[0, hd], kc, _NT, preferred_element_type=F32)
                              + cn_ref[:, hd:hd + 1] - cct_ref[hd:hd + 1, pl.ds(p0, pc)])
    attend(s_ref[...], lambda hd: vc_ref.at[0, 0][pl.ds(hd, pc, stride=nh), :].astype(BF16))

    @pl.when(step == pl.num_programs(1) - 1)
    def _():
        visible = (lax.broadcasted_iota(jnp.int32, (t, pad), 1)
                   <= lax.broadcasted_iota(jnp.int32, (t, pad), 0))
        zeros = jnp.zeros((pad - t, dh), BF16)
        for hd in range(nh):
            kn = jnp.concatenate([kn_ref[0, hd], zeros], axis=0)
            s = (lax.dot_general(q_ref[0, hd], kn, _NT, preferred_element_type=F32)
                 + cn_ref[:, hd:hd + 1] - cnt_ref[hd:hd + 1, :])
            sn_ref[rows(hd), :] = jnp.where(visible, s, -jnp.inf)
        attend(sn_ref[...], lambda hd: jnp.concatenate([vn_ref[0, hd], zeros], axis=0))
        o = (acc_ref[...] / l_ref[...]).astype(BF16)
        for hd in range(nh):
            o_ref[:, hd * dh:(hd + 1) * dh] = o[rows(hd), :]


def _fox_sample_attn(q, kn, vn, k_cache, v_cache, layer, logf_cache_t, logf_new, logf_new_t, *, batch):
    _, nh, n, dh = q.shape
    t = n // batch
    past = k_cache.shape[2] // nh
    pc = CACHE_CHUNK
    assert past % pc == 0 and past % CUMSUM_CHUNK == 0 and dh == LANES
    new_spec = pl.BlockSpec((1, nh, t, dh), lambda bi, si: (0, 0, bi, 0))
    cache_spec = pl.BlockSpec((1, 1, pc * nh, dh), lambda bi, si: (layer, bi, si, 0))
    whole = lambda shape: pl.BlockSpec((1,) + shape[1:], lambda bi, si: (bi, 0, 0))
    pad = logf_new_t.shape[2]
    return pl.pallas_call(
        _fox_sample_attn_kernel,
        grid=(batch, past // pc),
        in_specs=[new_spec, cache_spec, cache_spec, new_spec, new_spec,
                  whole(logf_cache_t.shape), whole(logf_new.shape), whole(logf_new_t.shape)],
        out_specs=pl.BlockSpec((t, nh * dh), lambda bi, si: (bi, 0)),
        out_shape=jax.ShapeDtypeStruct((n, nh * dh), BF16),
        scratch_shapes=[pltpu.VMEM((nh, past), F32), pltpu.VMEM((nh, pad), F32),
                        pltpu.VMEM((t, nh), F32), pltpu.VMEM((nh * t, 1), F32),
                        pltpu.VMEM((nh * t, 1), F32), pltpu.VMEM((nh * t, dh), F32),
                        pltpu.VMEM((nh * t, pc), F32), pltpu.VMEM((nh * t, pad), F32),
                        pltpu.VMEM((nh * t, dh), F32)],
        compiler_params=_params(2),
        name="fox_sample_attn",
    )(q, k_cache, v_cache, kn, vn, logf_cache_t, logf_new, logf_new_t)


def _fox_post_kernel(o_ref, sz_ref, x_ref, p_ref, gpost_ref, wout_ref, wgate_ref, wproj_ref, out_ref):
    o = _dot(o_ref[...] * sz_ref[...], wout_ref[...])
    out_ref[...] = _residual_and_embedding(x_ref[...], o, p_ref[...], gpost_ref, wgate_ref, wproj_ref)


def _fox_post(o, sz, x, p, gpost, w_out, w_gate, w_proj, *, layer, tokens):
    n, d = x.shape
    e = o.shape[1]
    assert n % tokens == 0
    row = lambda i: (i, 0)
    return pl.pallas_call(
        _fox_post_kernel,
        grid=(n // tokens,),
        in_specs=[pl.BlockSpec((tokens, e), row), pl.BlockSpec((tokens, e), row),
                  pl.BlockSpec((tokens, d), row),
                  pl.BlockSpec((None, tokens, p.shape[2]), lambda i: (layer, i, 0)),
                  _resident(gpost.shape), _resident(w_out.shape), _resident(w_gate.shape),
                  _resident(w_proj.shape)],
        out_specs=pl.BlockSpec((tokens, d), row),
        out_shape=jax.ShapeDtypeStruct((n, d), F32),
        compiler_params=_params(1),
        name="fox_post",
    )(o, sz, x, p, gpost, w_out, w_gate, w_proj)


def _mix_operands(w_s, b_s, rows, gd):
    n = min(rows, GMLP_BLOCK)
    reps = GMLP_BLOCK // n
    w = jnp.tile(w_s[:, :n, :n], (1, reps, reps))
    b = jnp.repeat(jnp.tile(b_s[:, :n], (1, reps)).T, gd, axis=1)
    return w, b, n


def kernel(x_prompt, x_sample, cache_fox_k, cache_fox_v, cache_fox_logf, p_prompt, p_sample, norm_pre, norm_post, gmlp_w_in, gmlp_ln_g, gmlp_ln_b, gmlp_w_s, gmlp_b_s, gmlp_w_out, fox_w_in, fox_b_f, fox_w_out, ple_w_proj, ple_w_gate):
    bp, sp, d = x_prompt.shape
    bs, ts, _ = x_sample.shape
    depth = norm_pre.shape[0]
    n_fox, _, past, nh, dh = cache_fox_k.shape
    e = nh * dh
    groups = gmlp_w_s.shape[1]
    gd = e // groups
    n_p, n_s = bp * sp, bs * ts

    xp = x_prompt.reshape(n_p, d)
    xs = x_sample.reshape(n_s, d)
    cache_k = cache_fox_k.reshape(cache_fox_k.shape[0], bs, past * nh, dh)
    cache_v = cache_fox_v.reshape(cache_fox_v.shape[0], bs, past * nh, dh)
    pp = p_prompt.reshape(depth, n_p, -1)
    ps = p_sample.reshape(depth, n_s, -1)
    gmlp_v_s = []
    flf_p, flf_s = [], []
    kv_p = None
    kv_s = (jnp.zeros((n_fox, 1, n_s * nh, dh), F32),) * 2
    for i in range(depth):
        j = i // 2
        gpre, gpost = norm_pre[i][None], norm_post[i][None]
        w_gate, w_proj = ple_w_gate[i].astype(BF16), ple_w_proj[i].astype(BF16)
        if i % 2 == 0:
            w_in, w_out = gmlp_w_in[j].astype(BF16), gmlp_w_out[j].astype(BF16)
            ln_g, ln_b = gmlp_ln_g[j][None], gmlp_ln_b[j][None]
            wm_p, bm_p, _ = _mix_operands(gmlp_w_s[j], gmlp_b_s[j], sp, gd)
            wm_s, bm_s, n_mix = _mix_operands(gmlp_w_s[j], gmlp_b_s[j], ts, gd)
            leaves = [(n_fox * n_p * nh, dh)] * 2 if kv_p is None else []
            xp, _, *zeroed = _gmlp_layer(xp, pp, gpre, gpost, w_in, ln_g, ln_b, wm_p, bm_p, w_out, w_gate,
                                         w_proj, layer=i, tokens=GMLP_TOKENS, span=CHUNK, causal=True,
                                         emit_v=False, zero_buffers=leaves)
            if zeroed:
                kv_p = [z.reshape(n_fox, bp, sp * nh, dh) for z in zeroed]
            assert n_mix <= CHUNK and n_s % GMLP_BLOCK == 0
            xs, v_s = _gmlp_layer(xs, ps, gpre, gpost, w_in, ln_g, ln_b, wm_s, bm_s, w_out, w_gate, w_proj,
                                  layer=i, tokens=GMLP_BLOCK, span=n_mix, causal=False, emit_v=True)
            gmlp_v_s.append(v_s.reshape(bs, ts, e))
        else:
            w = fox_w_in[j][:, :4 * e].astype(BF16)
            wf = fox_w_in[j][:, 4 * e:].astype(BF16)
            bf = fox_b_f[j][None]
            w_out = fox_w_out[j].astype(BF16)

            q, kb, vt, *kv_p, sz, lf = _fox_proj(xp, gpre, w, wf, bf, kv_p, layer=j, groups=bp, rows=sp,
                                                 heads=nh, tokens=PROJ_TOKENS, folded_bias=True)
            o = _fox_attn(q, kb, vt)
            xp = _fox_post(o.reshape(n_p, e), sz, xp, pp, gpost, w_out, w_gate, w_proj, layer=i,
                           tokens=POST_TOKENS)
            flf_p.append(lf.reshape(bp, sp, nh))

            q, kb, vb, *kv_s, sz, lf = _fox_proj(xs, gpre, w, wf, bf, kv_s, layer=j, groups=1, rows=n_s,
                                                 heads=nh, tokens=n_s, folded_bias=False)
            lf3 = lf.reshape(bs, ts, nh)
            lf3_t = jnp.pad(lf3.transpose(0, 2, 1), ((0, 0), (0, 0), (0, LANES - ts)))
            o = _fox_sample_attn(q, kb, vb, cache_k, cache_v, j, cache_fox_logf[j].transpose(0, 2, 1),
                                 lf3, lf3_t, batch=bs)
            xs = _fox_post(o, sz, xs, ps, gpost, w_out, w_gate, w_proj, layer=i, tokens=n_s)
            flf_s.append(lf3)
    fk_p, fv_p = (a.reshape(n_fox, bp, sp, nh, dh) for a in kv_p)
    fk_s, fv_s = (a.reshape(n_fox, bs, ts, nh, dh) for a in kv_s)
    return (xp.reshape(bp, sp, d), xs.reshape(bs, ts, d), jnp.stack(gmlp_v_s),
            fk_p, fv_p, jnp.stack(flf_p), fk_s, fv_s, jnp.stack(flf_s))
```

```python
import functools

import jax
import jax.numpy as jnp
from jax import lax
from jax.experimental import pallas as pl
from jax.experimental.pallas import tpu as pltpu

F32 = jnp.float32
BF16 = jnp.bfloat16

CHUNK = 64
GMLP_BLOCK = 128
RMS_EPS = 1e-6
LN_EPS = 1e-5
LANES = 128
VMEM_LIMIT_BYTES = 56 * 1024 * 1024

GMLP_TOKENS = 256
PROJ_TOKENS = 256
POST_TOKENS = 1024
ATTN_QUERIES = 512
ATTN_KEYS = 512
ATTN_CAUSAL_BLOCK = 256
ATTN_HEADS = 4
CACHE_CHUNK = 256
LOG2E = 1.4426950408889634
CUMSUM_CHUNK = 256

_NT = (((1,), (1,)), ((), ()))


def _params(n_axes):
    return pltpu.CompilerParams(
        dimension_semantics=("arbitrary",) * n_axes, vmem_limit_bytes=VMEM_LIMIT_BYTES)


def _resident(shape):
    zeros = (0,) * len(shape)
    return pl.BlockSpec(shape, lambda *_: zeros, pipeline_mode=pl.Buffered(1))


def _dot(a, b):
    return jnp.dot(a, b, preferred_element_type=F32)


def _rms_norm(x, g):
    return x * lax.rsqrt(jnp.mean(x * x, axis=-1, keepdims=True) + RMS_EPS) * g


def _silu(z):
    return z * jax.nn.sigmoid(z)


def _split3(x):
    a = x.astype(BF16)
    r = x - a.astype(F32)
    b = r.astype(BF16)
    c = (r - b.astype(F32)).astype(BF16)
    return a, b, c


def _dot_ones_lhs(ones, x):
    a, b, c = _split3(x)
    return (_dot(ones, a) + _dot(ones, b)) + _dot(ones, c)


def _dot_ones_rhs(x, ones):
    a, b, c = _split3(x)
    return (_dot(a, ones) + _dot(b, ones)) + _dot(c, ones)


def _tri(n, upper):
    r = lax.broadcasted_iota(jnp.int32, (n, n), 0)
    c = lax.broadcasted_iota(jnp.int32, (n, n), 1)
    keep = (r <= c) if upper else (c <= r)
    return jnp.where(keep, 1.0, 0.0).astype(BF16)


def _residual_and_embedding(x, o, p, gpost_ref, wgate_ref, wproj_ref):
    x1 = x + _rms_norm(o, gpost_ref[...])
    gate = jax.nn.sigmoid(_dot(x1.astype(BF16), wgate_ref[...]))
    return x1 + gate * _dot(p.astype(BF16), wproj_ref[...])


def _gmlp_layer_kernel(x_ref, p_ref, gpre_ref, gpost_ref, win_ref, lng_ref, lnb_ref,
                       wmix_ref, bmix_ref, wout_ref, wgate_ref, wproj_ref, *rest,
                       span_shift, causal, emit_v):
    out_ref, *extra, y_ref = rest
    v_ref = extra.pop(0) if emit_v else None
    for zero_ref in extra:
        zero_ref[...] = jnp.zeros_like(zero_ref)
    tm = x_ref.shape[0]
    e = wout_ref.shape[0]
    groups, blk, _ = wmix_ref.shape
    gd = e // groups

    x = x_ref[...]
    h = _rms_norm(x, gpre_ref[...]).astype(BF16)
    u = jax.nn.gelu(_dot(h, win_ref[:, 0:e]))
    gv = jax.nn.gelu(_dot(h, win_ref[:, e:2 * e]))
    sz = _silu(_dot(h, win_ref[:, 2 * e:3 * e]))
    d = gv - jnp.mean(gv, axis=-1, keepdims=True)
    v = d * lax.rsqrt(jnp.mean(d * d, axis=-1, keepdims=True) + LN_EPS) * lng_ref[...] + lnb_ref[...]
    if emit_v:
        v_ref[...] = v
    vb = v.astype(BF16)

    qs = lax.shift_right_logical(lax.broadcasted_iota(jnp.int32, (blk, blk), 0), span_shift)
    ks = lax.shift_right_logical(lax.broadcasted_iota(jnp.int32, (blk, blk), 1), span_shift)
    allowed = (ks <= qs) if causal else (ks == qs)
    row_blocks = [slice(r * blk, (r + 1) * blk) for r in range(tm // blk)]
    for g in range(groups):
        cs = slice(g * gd, (g + 1) * gd)
        wm = jnp.where(allowed, wmix_ref[g], 0.0).astype(BF16)
        mixed = _dot(wm, jnp.concatenate([vb[rs, cs] for rs in row_blocks], axis=1))
        for r, rs in enumerate(row_blocks):
            s = mixed[:, r * gd:(r + 1) * gd] + bmix_ref[:, cs]
            y_ref[rs, cs] = (u[rs, cs] * s * sz[rs, cs]).astype(BF16)
    o = _dot(y_ref[...], wout_ref[...])
    out_ref[...] = _residual_and_embedding(x, o, p_ref[...], gpost_ref, wgate_ref, wproj_ref)


def _gmlp_layer(x, p, gpre, gpost, w_in, ln_g, ln_b, w_mix, b_mix, w_out, w_gate, w_proj,
                *, layer, tokens, span, causal, emit_v, zero_buffers=()):
    n, d = x.shape
    e = w_out.shape[0]
    pd = p.shape[2]
    assert n % tokens == 0 and tokens % GMLP_BLOCK == 0 and span & (span - 1) == 0
    row = lambda i: (i, 0)
    out_shape = [jax.ShapeDtypeStruct((n, d), F32)]
    out_specs = [pl.BlockSpec((tokens, d), row)]
    if emit_v:
        out_shape.append(jax.ShapeDtypeStruct((n, e), F32))
        out_specs.append(pl.BlockSpec((tokens, e), row))
    steps = n // tokens
    for rows, lanes in zero_buffers:
        assert rows % steps == 0
        out_shape.append(jax.ShapeDtypeStruct((rows, lanes), F32))
        out_specs.append(pl.BlockSpec((rows // steps, lanes), row))
    res = pl.pallas_call(
        functools.partial(_gmlp_layer_kernel, span_shift=span.bit_length() - 1,
                          causal=causal, emit_v=emit_v),
        grid=(steps,),
        in_specs=[
            pl.BlockSpec((tokens, d), row),
            pl.BlockSpec((None, tokens, pd), lambda i: (layer, i, 0)),
            _resident(gpre.shape), _resident(gpost.shape), _resident(w_in.shape),
            _resident(ln_g.shape), _resident(ln_b.shape), _resident(w_mix.shape),
            _resident(b_mix.shape), _resident(w_out.shape), _resident(w_gate.shape),
            _resident(w_proj.shape),
        ],
        out_specs=out_specs,
        out_shape=out_shape,
        scratch_shapes=[pltpu.VMEM((tokens, e), BF16)],
        compiler_params=_params(1),
        name="gmlp_layer",
    )(x, p, gpre, gpost, w_in, ln_g, ln_b, w_mix, b_mix, w_out, w_gate, w_proj)
    return (res[0], res[1] if emit_v else None, *res[2 if emit_v else 1:])


def _bias_columns(c, hd, width):
    hi, mid, lo = _split3(c[:, hd:hd + 1])
    shape = (c.shape[0], width)
    return tuple(jnp.broadcast_to(t.astype(F32), shape) for t in (hi, mid, lo))


def _fox_proj_kernel(x_ref, gpre_ref, w_ref, wf_ref, bf_ref, kf_in_ref, vf_in_ref,
                     q_ref, k_ref, v_ref, kf_ref, vf_ref, sz_ref, logf_ref, *scratch, scale, folded_bias):
    del kf_in_ref, vf_in_ref
    tm, e = sz_ref.shape
    nh = q_ref.shape[1]
    dh = e // nh
    heads = [(hd, slice(hd * dh, (hd + 1) * dh)) for hd in range(nh)]
    h = _rms_norm(x_ref[...], gpre_ref[...]).astype(BF16)
    logf = jax.nn.log_sigmoid(_dot(h, wf_ref[...]) + bf_ref[...])
    logf_ref[...] = logf

    if folded_bias:
        (carry_ref,) = scratch

        @pl.when(pl.program_id(1) == 0)
        def _():
            carry_ref[...] = jnp.zeros_like(carry_ref)

        c = _dot_ones_lhs(_tri(tm, upper=False), logf) + carry_ref[...]
        carry_ref[...] = c[tm - 1:tm, :]
        c = c * LOG2E
        lane = lax.broadcasted_iota(jnp.int32, (tm, dh), 1)
        third = lane - jnp.where(lane >= 3, 3, 0)
        in_q, in_k = lane < 3, (lane >= 3) & (lane < 6)
        ones_q = jnp.where(in_k, 1.0, 0.0)
        ones_k = jnp.where(in_q, 1.0, 0.0)
        for hd, _ in heads:
            hi, mid, lo = _bias_columns(c, hd, dh)
            terms = jnp.where(third == 0, hi, jnp.where(third == 1, mid, lo))
            q_ref[0, hd, dh:2 * dh, :] = jnp.where(in_q, terms, ones_q).T.astype(BF16)
            k_ref[0, hd, :, dh:2 * dh] = jnp.where(in_k, -terms, ones_k).astype(BF16)

    q = _dot(h, w_ref[:, 0:e]) * scale
    for hd, hs in heads:
        if folded_bias:
            q_ref[0, hd, 0:dh, :] = q[:, hs].T.astype(BF16)
        else:
            q_ref[0, hd] = q[:, hs].astype(BF16)
    k = _dot(h, w_ref[:, e:2 * e])
    for hd, hs in heads:
        kf_ref.at[0, 0][pl.ds(hd, tm, stride=nh), :] = k[:, hs]
        if folded_bias:
            k_ref[0, hd, :, 0:dh] = k[:, hs].astype(BF16)
        else:
            k_ref[0, hd] = k[:, hs].astype(BF16)
    v = _dot(h, w_ref[:, 2 * e:3 * e])
    for hd, hs in heads:
        vf_ref.at[0, 0][pl.ds(hd, tm, stride=nh), :] = v[:, hs]
        v_ref[0, hd] = (v[:, hs].T if folded_bias else v[:, hs]).astype(BF16)
    sz_ref[...] = _silu(_dot(h, w_ref[:, 3 * e:4 * e])).astype(BF16)


def _fox_proj(x, gpre, w, wf, bf, kv_leaves, *, layer, groups, rows, heads, tokens, folded_bias):
    n, d = x.shape
    e = (w.shape[1]) // 4
    dh = e // heads
    assert n == groups * rows and rows % tokens == 0
    steps = rows // tokens
    tok = lambda b, j: (b * steps + j, 0)
    if folded_bias:
        q_shape = jax.ShapeDtypeStruct((groups, heads, 2 * dh, rows), BF16)
        q_spec = pl.BlockSpec((1, heads, 2 * dh, tokens), lambda b, j: (b, 0, 0, j))
        k_shape = jax.ShapeDtypeStruct((groups, heads, rows, 2 * dh), BF16)
        k_spec = pl.BlockSpec((1, heads, tokens, 2 * dh), lambda b, j: (b, 0, j, 0))
        v_shape = jax.ShapeDtypeStruct((groups, heads, dh, rows), BF16)
        v_spec = pl.BlockSpec((1, heads, dh, tokens), lambda b, j: (b, 0, 0, j))
        scratch = [pltpu.VMEM((1, heads), F32)]
        scale = float(dh) ** -0.5 * LOG2E
    else:
        q_shape = k_shape = v_shape = jax.ShapeDtypeStruct((groups, heads, rows, dh), BF16)
        q_spec = k_spec = v_spec = pl.BlockSpec((1, heads, tokens, dh), lambda b, j: (b, 0, j, 0))
        scratch = []
        scale = float(dh) ** -0.5
    k_leaf, v_leaf = kv_leaves
    assert k_leaf.shape == v_leaf.shape == (k_leaf.shape[0], groups, rows * heads, dh)
    leaf_shape = jax.ShapeDtypeStruct(k_leaf.shape, F32)
    leaf_spec = pl.BlockSpec((1, 1, tokens * heads, dh), lambda b, j: (layer, b, j, 0))
    return pl.pallas_call(
        functools.partial(_fox_proj_kernel, scale=scale, folded_bias=folded_bias),
        grid=(groups, steps),
        in_specs=[pl.BlockSpec((tokens, d), tok), _resident(gpre.shape), _resident(w.shape),
                  _resident(wf.shape), _resident(bf.shape),
                  pl.BlockSpec(memory_space=pl.ANY), pl.BlockSpec(memory_space=pl.ANY)],
        out_specs=[q_spec, k_spec, v_spec, leaf_spec, leaf_spec,
                   pl.BlockSpec((tokens, e), tok), pl.BlockSpec((tokens, heads), tok)],
        out_shape=[q_shape, k_shape, v_shape, leaf_shape, leaf_shape,
                   jax.ShapeDtypeStruct((n, e), BF16), jax.ShapeDtypeStruct((n, heads), F32)],
        input_output_aliases={5: 3, 6: 4},
        scratch_shapes=scratch,
        compiler_params=_params(2),
        name="fox_proj",
    )(x, gpre, w, wf, bf, k_leaf, v_leaf)


def _fox_attn_kernel(qt_ref, qt_next_ref, k_ref, vt_ref, o_ref, s_ref, smax_ref, m_ref, l_ref, acc_ref):
    hpb, tq = qt_ref.shape[1], qt_ref.shape[3]
    tk = s_ref.shape[2]
    qi = pl.program_id(2)

    def produce(j, slot, hh, queries_ref=qt_ref):
        k0 = pl.multiple_of(j * tk, tk)
        s = _dot(k_ref[0, hh, pl.ds(k0, tk), :], queries_ref[0, hh])
        s_ref[slot, hh] = s
        smax_ref[slot, hh] = jnp.max(s, axis=0, keepdims=True)

    def consume_full(j, slot, hh):
        k0 = pl.multiple_of(j * tk, tk)
        m = m_ref[hh]
        m_new = jnp.maximum(m, smax_ref[slot, hh])
        alpha = jnp.exp2(m - m_new)
        p = jnp.exp2(s_ref[slot, hh] - m_new)
        m_ref[hh] = m_new
        l_ref[hh] = alpha * l_ref[hh] + jnp.sum(p, axis=0, keepdims=True)
        acc_ref[hh] = alpha * acc_ref[hh] + _dot(vt_ref[0, hh, :, pl.ds(k0, tk)], p.astype(BF16))

    def consume_diagonal(j, slot, hh):
        k0 = pl.multiple_of(j * tk, tk)
        sub = ATTN_CAUSAL_BLOCK
        on_or_below = (lax.broadcasted_iota(jnp.int32, (sub, sub), 0)
                       <= lax.broadcasted_iota(jnp.int32, (sub, sub), 1))
        for b in range(tq // sub):
            cols = slice(b * sub, (b + 1) * sub)
            edge = jnp.where(on_or_below, s_ref[slot, hh, b * sub:(b + 1) * sub, cols], -jnp.inf)
            s_max = jnp.max(edge, axis=0, keepdims=True)
            if b:
                whole = s_ref[slot, hh, 0:b * sub, cols]
                s_max = jnp.maximum(s_max, jnp.max(whole, axis=0, keepdims=True))
            m = m_ref[hh, :, cols]
            m_new = jnp.maximum(m, s_max)
            alpha = jnp.exp2(m - m_new)
            p = jnp.exp2(edge - m_new)
            l_new = jnp.sum(p, axis=0, keepdims=True)
            pv = _dot(vt_ref[0, hh, :, pl.ds(pl.multiple_of(k0 + b * sub, sub), sub)], p.astype(BF16))
            if b:
                p = jnp.exp2(whole - m_new)
                l_new = l_new + jnp.sum(p, axis=0, keepdims=True)
                pv = pv + _dot(vt_ref[0, hh, :, pl.ds(k0, b * sub)], p.astype(BF16))
            m_ref[hh, :, cols] = m_new
            l_ref[hh, :, cols] = alpha * l_ref[hh, :, cols] + l_new
            acc_ref[hh, :, cols] = alpha * acc_ref[hh, :, cols] + pv

    def consume(j, slot, hh, diagonal):
        (consume_diagonal if diagonal else consume_full)(j, slot, hh)

    def step(produced, consumed):
        for hh in range(hpb):
            if produced is not None:
                produce(produced[0], produced[1], hh, *produced[2:])
            if consumed is not None:
                consume(*consumed[:2], hh, consumed[2])

    m_ref[...] = jnp.full_like(m_ref, -jnp.inf)
    l_ref[...] = jnp.zeros_like(l_ref)
    acc_ref[...] = jnp.zeros_like(acc_ref)
    n_full = qi
    first = s_ref.shape[0] - 1
    hand_over = (0, first, qt_next_ref)

    @pl.when(qi == 0)
    def _():
        step((0, 0), None)
        step(hand_over, (0, 0, True))

    @pl.when(qi > 0)
    def _():
        step((1, 1), (0, first, False))

        @pl.loop(0, (n_full - 1) // 2)
        def _(i):
            step((2 * i + 2, 0), (2 * i + 1, 1, False))
            step((2 * i + 3, 1), (2 * i + 2, 0, False))

        @pl.when(n_full % 2 == 1)
        def _():
            step(hand_over, (n_full, 1, True))

        @pl.when(n_full % 2 == 0)
        def _():
            step((n_full, 0), (n_full - 1, 1, False))
            step(hand_over, (n_full, 0, True))

    dh = acc_ref.shape[1]
    for hh in range(hpb):
        o_ref[0, :, hh * dh:(hh + 1) * dh] = (acc_ref[hh] / l_ref[hh]).T.astype(BF16)


def _fox_attn(qt, k, vt):
    b, nh, s, dh2 = k.shape
    dh = vt.shape[2]
    t, tk, hpb = ATTN_QUERIES, ATTN_KEYS, ATTN_HEADS
    assert s % tk == 0 and tk == t and t % ATTN_CAUSAL_BLOCK == 0
    assert nh % hpb == 0 and dh == LANES and dh2 == 2 * dh
    last = s // t - 1
    return pl.pallas_call(
        _fox_attn_kernel,
        grid=(b, nh // hpb, s // t),
        in_specs=[
            pl.BlockSpec((1, hpb, dh2, t), lambda bi, hi, qi: (bi, hi, 0, qi)),
            pl.BlockSpec((1, hpb, dh2, t), lambda bi, hi, qi: (bi, hi, 0, jnp.minimum(qi + 1, last))),
            pl.BlockSpec((1, hpb, s, dh2), lambda bi, hi, qi: (bi, hi, 0, 0)),
            pl.BlockSpec((1, hpb, dh, s), lambda bi, hi, qi: (bi, hi, 0, 0)),
        ],
        out_specs=pl.BlockSpec((1, t, hpb * dh), lambda bi, hi, qi: (bi, qi, hi)),
        out_shape=jax.ShapeDtypeStruct((b, s, nh * dh), BF16),
        scratch_shapes=[pltpu.VMEM((3, hpb, tk, t), F32), pltpu.VMEM((3, hpb, 1, t), F32),
                        pltpu.VMEM((hpb, 1, t), F32), pltpu.VMEM((hpb, 1, t), F32),
                        pltpu.VMEM((hpb, dh, t), F32)],
        compiler_params=_params(3),
        name="fox_attn",
    )(qt, qt, k, vt)


def _fox_sample_attn_kernel(q_ref, kc_ref, vc_ref, kn_ref, vn_ref, lfct_ref, lfn_ref, lfnt_ref,
                            o_ref, cct_ref, cnt_ref, cn_ref, m_ref, l_ref, acc_ref, s_ref, sn_ref, pv_ref):
    nh, t, dh = q_ref.shape[1], q_ref.shape[2], q_ref.shape[3]
    past = lfct_ref.shape[2]
    pad = lfnt_ref.shape[2]
    pc = kc_ref.shape[2] // nh
    step = pl.program_id(1)

    @pl.when(step == 0)
    def _():
        triu = _tri(CUMSUM_CHUNK, upper=True)
        total = jnp.zeros((nh, 1), F32)
        for i in range(past // CUMSUM_CHUNK):
            cs = slice(i * CUMSUM_CHUNK, (i + 1) * CUMSUM_CHUNK)
            blk = _dot_ones_rhs(lfct_ref[0, :, cs], triu) + total
            cct_ref[:, cs] = blk
            total = blk[:, CUMSUM_CHUNK - 1:CUMSUM_CHUNK]
        eye = (lax.broadcasted_iota(jnp.int32, (nh, nh), 0)
               == lax.broadcasted_iota(jnp.int32, (nh, nh), 1))
        total_row = jnp.sum(jnp.where(eye, total, 0.0), axis=0, keepdims=True)
        cn_ref[...] = _dot_ones_lhs(_tri(t, upper=False), lfn_ref[0]) + total_row
        cnt_ref[...] = _dot_ones_rhs(lfnt_ref[0], _tri(pad, upper=True)) + total
        m_ref[...] = jnp.full_like(m_ref, -jnp.inf)
        l_ref[...] = jnp.zeros_like(l_ref)
        acc_ref[...] = jnp.zeros_like(acc_ref)

    def rows(hd):
        return slice(hd * t, (hd + 1) * t)

    def attend(s, values):
        m = m_ref[...]
        m_new = jnp.maximum(m, jnp.max(s, axis=1, keepdims=True))
        alpha = jnp.exp(m - m_new)
        p = jnp.exp(s - m_new)
        m_ref[...] = m_new
        l_ref[...] = alpha * l_ref[...] + jnp.sum(p, axis=1, keepdims=True)
        p = p.astype(BF16)
        for hd in range(nh):
            pv_ref[rows(hd), :] = _dot(p[rows(hd), :], values(hd))
        acc_ref[...] = alpha * acc_ref[...] + pv_ref[...]

    p0 = pl.multiple_of(step * pc, pc)
    for hd in range(nh):
        kc = kc_ref.at[0, 0][pl.ds(hd, pc, stride=nh), :].astype(BF16)
        s_ref[rows(hd), :] = (lax.dot_general(q_ref[0, hd], kc, _NT, preferred_element_type=F32)
                              + cn_ref[:, hd:hd + 1] - cct_ref[hd:hd + 1, pl.ds(p0, pc)])
    attend(s_ref[...], lambda hd: vc_ref.at[0, 0][pl.ds(hd, pc, stride=nh), :].astype(BF16))

    @pl.when(step == pl.num_programs(1) - 1)
    def _():
        visible = (lax.broadcasted_iota(jnp.int32, (t, pad), 1)
                   <= lax.broadcasted_iota(jnp.int32, (t, pad), 0))
        zeros = jnp.zeros((pad - t, dh), BF16)
        for hd in range(nh):
            kn = jnp.concatenate([kn_ref[0, hd], zeros], axis=0)
            s = (lax.dot_general(q_ref[0, hd], kn, _NT, preferred_element_type=F32)
                 + cn_ref[:, hd:hd + 1] - cnt_ref[hd:hd + 1, :])
            sn_ref[rows(hd), :] = jnp.where(visible, s, -jnp.inf)
        attend(sn_ref[...], lambda hd: jnp.concatenate([vn_ref[0, hd], zeros], axis=0))
        o = (acc_ref[...] / l_ref[...]).astype(BF16)
        for hd in range(nh):
            o_ref[:, hd * dh:(hd + 1) * dh] = o[rows(hd), :]


def _fox_sample_attn(q, kn, vn, k_cache, v_cache, layer, logf_cache_t, logf_new, logf_new_t, *, batch):
    _, nh, n, dh = q.shape
    t = n // batch
    past = k_cache.shape[2] // nh
    pc = CACHE_CHUNK
    assert past % pc == 0 and past % CUMSUM_CHUNK == 0 and dh == LANES
    new_spec = pl.BlockSpec((1, nh, t, dh), lambda bi, si: (0, 0, bi, 0))
    cache_spec = pl.BlockSpec((1, 1, pc * nh, dh), lambda bi, si: (layer, bi, si, 0))
    whole = lambda shape: pl.BlockSpec((1,) + shape[1:], lambda bi, si: (bi, 0, 0))
    pad = logf_new_t.shape[2]
    return pl.pallas_call(
        _fox_sample_attn_kernel,
        grid=(batch, past // pc),
        in_specs=[new_spec, cache_spec, cache_spec, new_spec, new_spec,
                  whole(logf_cache_t.shape), whole(logf_new.shape), whole(logf_new_t.shape)],
        out_specs=pl.BlockSpec((t, nh * dh), lambda bi, si: (bi, 0)),
        out_shape=jax.ShapeDtypeStruct((n, nh * dh), BF16),
        scratch_shapes=[pltpu.VMEM((nh, past), F32), pltpu.VMEM((nh, pad), F32),
                        pltpu.VMEM((t, nh), F32), pltpu.VMEM((nh * t, 1), F32),
                        pltpu.VMEM((nh * t, 1), F32), pltpu.VMEM((nh * t, dh), F32),
                        pltpu.VMEM((nh * t, pc), F32), pltpu.VMEM((nh * t, pad), F32),
                        pltpu.VMEM((nh * t, dh), F32)],
        compiler_params=_params(2),
        name="fox_sample_attn",
    )(q, k_cache, v_cache, kn, vn, logf_cache_t, logf_new, logf_new_t)


def _fox_post_kernel(o_ref, sz_ref, x_ref, p_ref, gpost_ref, wout_ref, wgate_ref, wproj_ref, out_ref):
    o = _dot(o_ref[...] * sz_ref[...], wout_ref[...])
    out_ref[...] = _residual_and_embedding(x_ref[...], o, p_ref[...], gpost_ref, wgate_ref, wproj_ref)


def _fox_post(o, sz, x, p, gpost, w_out, w_gate, w_proj, *, layer, tokens):
    n, d = x.shape
    e = o.shape[1]
    assert n % tokens == 0
    row = lambda i: (i, 0)
    return pl.pallas_call(
        _fox_post_kernel,
        grid=(n // tokens,),
        in_specs=[pl.BlockSpec((tokens, e), row), pl.BlockSpec((tokens, e), row),
                  pl.BlockSpec((tokens, d), row),
                  pl.BlockSpec((None, tokens, p.shape[2]), lambda i: (layer, i, 0)),
                  _resident(gpost.shape), _resident(w_out.shape), _resident(w_gate.shape),
                  _resident(w_proj.shape)],
        out_specs=pl.BlockSpec((tokens, d), row),
        out_shape=jax.ShapeDtypeStruct((n, d), F32),
        compiler_params=_params(1),
        name="fox_post",
    )(o, sz, x, p, gpost, w_out, w_gate, w_proj)


def _mix_operands(w_s, b_s, rows, gd):
    n = min(rows, GMLP_BLOCK)
    reps = GMLP_BLOCK // n
    w = jnp.tile(w_s[:, :n, :n], (1, reps, reps))
    b = jnp.repeat(jnp.tile(b_s[:, :n], (1, reps)).T, gd, axis=1)
    return w, b, n


def kernel(x_prompt, x_sample, cache_fox_k, cache_fox_v, cache_fox_logf, p_prompt, p_sample, norm_pre, norm_post, gmlp_w_in, gmlp_ln_g, gmlp_ln_b, gmlp_w_s, gmlp_b_s, gmlp_w_out, fox_w_in, fox_b_f, fox_w_out, ple_w_proj, ple_w_gate):
    bp, sp, d = x_prompt.shape
    bs, ts, _ = x_sample.shape
    depth = norm_pre.shape[0]
    n_fox, _, past, nh, dh = cache_fox_k.shape
    e = nh * dh
    groups = gmlp_w_s.shape[1]
    gd = e // groups
    n_p, n_s = bp * sp, bs * ts

    xp = x_prompt.reshape(n_p, d)
    xs = x_sample.reshape(n_s, d)
    cache_k = cache_fox_k.reshape(cache_fox_k.shape[0], bs, past * nh, dh)
    cache_v = cache_fox_v.reshape(cache_fox_v.shape[0], bs, past * nh, dh)
    pp = p_prompt.reshape(depth, n_p, -1)
    ps = p_sample.reshape(depth, n_s, -1)
    gmlp_v_s = []
    flf_p, flf_s = [], []
    kv_p = None
    kv_s = (jnp.zeros((n_fox, 1, n_s * nh, dh), F32),) * 2
    for i in range(depth):
        j = i // 2
        gpre, gpost = norm_pre[i][None], norm_post[i][None]
        w_gate, w_proj = ple_w_gate[i].astype(BF16), ple_w_proj[i].astype(BF16)
        if i % 2 == 0:
            w_in, w_out = gmlp_w_in[j].astype(BF16), gmlp_w_out[j].astype(BF16)
            ln_g, ln_b = gmlp_ln_g[j][None], gmlp_ln_b[j][None]
            wm_p, bm_p, _ = _mix_operands(gmlp_w_s[j], gmlp_b_s[j], sp, gd)
            wm_s, bm_s, n_mix = _mix_operands(gmlp_w_s[j], gmlp_b_s[j], ts, gd)
            leaves = [(n_fox * n_p * nh, dh)] * 2 if kv_p is None else []
            xp, _, *zeroed = _gmlp_layer(xp, pp, gpre, gpost, w_in, ln_g, ln_b, wm_p, bm_p, w_out, w_gate,
                                         w_proj, layer=i, tokens=GMLP_TOKENS, span=CHUNK, causal=True,
                                         emit_v=False, zero_buffers=leaves)
            if zeroed:
                kv_p = [z.reshape(n_fox, bp, sp * nh, dh) for z in zeroed]
            assert n_mix <= CHUNK and n_s % GMLP_BLOCK == 0
            xs, v_s = _gmlp_layer(xs, ps, gpre, gpost, w_in, ln_g, ln_b, wm_s, bm_s, w_out, w_gate, w_proj,
                                  layer=i, tokens=GMLP_BLOCK, span=n_mix, causal=False, emit_v=True)
            gmlp_v_s.append(v_s.reshape(bs, ts, e))
        else:
            w = fox_w_in[j][:, :4 * e].astype(BF16)
            wf = fox_w_in[j][:, 4 * e:].astype(BF16)
            bf = fox_b_f[j][None]
            w_out = fox_w_out[j].astype(BF16)

            q, kb, vt, *kv_p, sz, lf = _fox_proj(xp, gpre, w, wf, bf, kv_p, layer=j, groups=bp, rows=sp,
                                                 heads=nh, tokens=PROJ_TOKENS, folded_bias=True)
            o = _fox_attn(q, kb, vt)
            xp = _fox_post(o.reshape(n_p, e), sz, xp, pp, gpost, w_out, w_gate, w_proj, layer=i,
                           tokens=POST_TOKENS)
            flf_p.append(lf.reshape(bp, sp, nh))

            q, kb, vb, *kv_s, sz, lf = _fox_proj(xs, gpre, w, wf, bf, kv_s, layer=j, groups=1, rows=n_s,
                                                 heads=nh, tokens=n_s, folded_bias=False)
            lf3 = lf.reshape(bs, ts, nh)
            lf3_t = jnp.pad(lf3.transpose(0, 2, 1), ((0, 0), (0, 0), (0, LANES - ts)))
            o = _fox_sample_attn(q, kb, vb, cache_k, cache_v, j, cache_fox_logf[j].transpose(0, 2, 1),
                                 lf3, lf3_t, batch=bs)
            xs = _fox_post(o, sz, xs, ps, gpost, w_out, w_gate, w_proj, layer=i, tokens=n_s)
            flf_s.append(lf3)
    fk_p, fv_p = (a.reshape(n_fox, bp, sp, nh, dh) for a in kv_p)
    fk_s, fv_s = (a.reshape(n_fox, bs, ts, nh, dh) for a in kv_s)
    return (xp.reshape(bp, sp, d), xs.reshape(bs, ts, d), jnp.stack(gmlp_v_s),
            fk_p, fv_p, jnp.stack(flf_p), fk_s, fv_s, jnp.stack(flf_s))
```

```python
import functools

import jax
import jax.numpy as jnp
from jax import lax
from jax.experimental import pallas as pl
from jax.experimental.pallas import tpu as pltpu

F32 = jnp.float32
BF16 = jnp.bfloat16

CHUNK = 64
GMLP_BLOCK = 128
RMS_EPS = 1e-6
LN_EPS = 1e-5
LANES = 128
VMEM_LIMIT_BYTES = 56 * 1024 * 1024

GMLP_TOKENS = 256
PROJ_TOKENS = 256
POST_TOKENS = 1024
ATTN_QUERIES = 512
ATTN_KEYS = 512
ATTN_CAUSAL_BLOCK = 256
ATTN_HEADS = 4
CACHE_CHUNK = 256
LOG2E = 1.4426950408889634
CUMSUM_CHUNK = 256

_NT = (((1,), (1,)), ((), ()))


def _params(n_axes):
    return pltpu.CompilerParams(
        dimension_semantics=("arbitrary",) * n_axes, vmem_limit_bytes=VMEM_LIMIT_BYTES)


def _resident(shape):
    zeros = (0,) * len(shape)
    return pl.BlockSpec(shape, lambda *_: zeros, pipeline_mode=pl.Buffered(1))


def _dot(a, b):
    return jnp.dot(a, b, preferred_element_type=F32)


def _rms_norm(x, g):
    return x * lax.rsqrt(jnp.mean(x * x, axis=-1, keepdims=True) + RMS_EPS) * g


def _silu(z):
    return z * jax.nn.sigmoid(z)


def _split3(x):
    a = x.astype(BF16)
    r = x - a.astype(F32)
    b = r.astype(BF16)
    c = (r - b.astype(F32)).astype(BF16)
    return a, b, c


def _dot_ones_lhs(ones, x):
    a, b, c = _split3(x)
    return (_dot(ones, a) + _dot(ones, b)) + _dot(ones, c)


def _dot_ones_rhs(x, ones):
    a, b, c = _split3(x)
    return (_dot(a, ones) + _dot(b, ones)) + _dot(c, ones)


def _tri(n, upper):
    r = lax.broadcasted_iota(jnp.int32, (n, n), 0)
    c = lax.broadcasted_iota(jnp.int32, (n, n), 1)
    keep = (r <= c) if upper else (c <= r)
    return jnp.where(keep, 1.0, 0.0).astype(BF16)


def _residual_and_embedding(x, o, p, gpost_ref, wgate_ref, wproj_ref):
    x1 = x + _rms_norm(o, gpost_ref[...])
    gate = jax.nn.sigmoid(_dot(x1.astype(BF16), wgate_ref[...]))
    return x1 + gate * _dot(p.astype(BF16), wproj_ref[...])


def _gmlp_layer_kernel(x_ref, p_ref, gpre_ref, gpost_ref, win_ref, lng_ref, lnb_ref,
                       wmix_ref, bmix_ref, wout_ref, wgate_ref, wproj_ref, *rest,
                       span_shift, causal, emit_v):
    out_ref, *extra, y_ref = rest
    v_ref = extra.pop(0) if emit_v else None
    tm = x_ref.shape[0]
    e = wout_ref.shape[0]
    groups, blk, _ = wmix_ref.shape
    gd = e // groups

    x = x_ref[...]
    hf = _rms_norm(x, gpre_ref[...])
    h = hf.astype(BF16)

    def after_zero_fill(part_index, parts):
        if not extra:
            return h
        anchor = jnp.zeros((1, LANES), F32)
        for zero_ref in extra:
            rows = zero_ref.shape[0] // parts
            r0 = part_index * rows
            zero_ref[r0:r0 + rows, :] = jnp.zeros((rows, zero_ref.shape[1]), zero_ref.dtype)
            row = pl.multiple_of(r0 + 8 * jnp.minimum(pl.program_id(0), 0), 8)
            anchor = anchor + zero_ref[pl.ds(row, 1), 0:LANES]
        return (hf + jnp.tile(anchor, (1, hf.shape[1] // LANES))).astype(BF16)

    u = jax.nn.gelu(_dot(h, win_ref[:, 0:e]))
    gv = jax.nn.gelu(_dot(after_zero_fill(0, 2), win_ref[:, e:2 * e]))
    sz = _silu(_dot(after_zero_fill(1, 2), win_ref[:, 2 * e:3 * e]))
    d = gv - jnp.mean(gv, axis=-1, keepdims=True)
    v = d * lax.rsqrt(jnp.mean(d * d, axis=-1, keepdims=True) + LN_EPS) * lng_ref[...] + lnb_ref[...]
    if emit_v:
        v_ref[...] = v
    vb = v.astype(BF16)

    qs = lax.shift_right_logical(lax.broadcasted_iota(jnp.int32, (blk, blk), 0), span_shift)
    ks = lax.shift_right_logical(lax.broadcasted_iota(jnp.int32, (blk, blk), 1), span_shift)
    allowed = (ks <= qs) if causal else (ks == qs)
    row_blocks = [slice(r * blk, (r + 1) * blk) for r in range(tm // blk)]
    for g in range(groups):
        cs = slice(g * gd, (g + 1) * gd)
        wm = jnp.where(allowed, wmix_ref[g], 0.0).astype(BF16)
        bias = bmix_ref[:, cs]
        mixed = _dot(wm, jnp.concatenate([vb[rs, cs] for rs in row_blocks], axis=1))
        for r, rs in enumerate(row_blocks):
            s = mixed[:, r * gd:(r + 1) * gd] + bias
            y_ref[rs, cs] = (u[rs, cs] * s * sz[rs, cs]).astype(BF16)
    o = _dot(y_ref[...], wout_ref[...])
    out_ref[...] = _residual_and_embedding(x, o, p_ref[...], gpost_ref, wgate_ref, wproj_ref)


def _gmlp_layer(x, p, gpre, gpost, w_in, ln_g, ln_b, w_mix, b_mix, w_out, w_gate, w_proj,
                *, layer, tokens, span, causal, emit_v, zero_buffers=()):
    n, d = x.shape
    e = w_out.shape[0]
    pd = p.shape[2]
    assert n % tokens == 0 and tokens % GMLP_BLOCK == 0 and span & (span - 1) == 0
    row = lambda i: (i, 0)
    out_shape = [jax.ShapeDtypeStruct((n, d), F32)]
    out_specs = [pl.BlockSpec((tokens, d), row)]
    if emit_v:
        out_shape.append(jax.ShapeDtypeStruct((n, e), F32))
        out_specs.append(pl.BlockSpec((tokens, e), row))
    steps = n // tokens
    for rows, lanes in zero_buffers:
        assert rows % steps == 0
        out_shape.append(jax.ShapeDtypeStruct((rows, lanes), F32))
        out_specs.append(pl.BlockSpec((rows // steps, lanes), row))
    res = pl.pallas_call(
        functools.partial(_gmlp_layer_kernel, span_shift=span.bit_length() - 1,
                          causal=causal, emit_v=emit_v),
        grid=(steps,),
        in_specs=[
            pl.BlockSpec((tokens, d), row),
            pl.BlockSpec((None, tokens, pd), lambda i: (layer, i, 0)),
            _resident(gpre.shape), _resident(gpost.shape), _resident(w_in.shape),
            _resident(ln_g.shape), _resident(ln_b.shape), _resident(w_mix.shape),
            _resident(b_mix.shape), _resident(w_out.shape), _resident(w_gate.shape),
            _resident(w_proj.shape),
        ],
        out_specs=out_specs,
        out_shape=out_shape,
        scratch_shapes=[pltpu.VMEM((tokens, e), BF16)],
        compiler_params=_params(1),
        name="gmlp_layer",
    )(x, p, gpre, gpost, w_in, ln_g, ln_b, w_mix, b_mix, w_out, w_gate, w_proj)
    return (res[0], res[1] if emit_v else None, *res[2 if emit_v else 1:])


def _bias_columns(c, hd, width):
    hi, mid, lo = _split3(c[:, hd:hd + 1])
    shape = (c.shape[0], width)
    return tuple(jnp.broadcast_to(t.astype(F32), shape) for t in (hi, mid, lo))


def _fox_proj_kernel(x_ref, gpre_ref, w_ref, wf_ref, bf_ref, kf_in_ref, vf_in_ref,
                     q_ref, k_ref, v_ref, kf_ref, vf_ref, sz_ref, logf_ref, *scratch, scale, folded_bias):
    del kf_in_ref, vf_in_ref
    tm, e = sz_ref.shape
    nh = q_ref.shape[1]
    dh = e // nh
    heads = [(hd, slice(hd * dh, (hd + 1) * dh)) for hd in range(nh)]
    h = _rms_norm(x_ref[...], gpre_ref[...]).astype(BF16)
    logf = jax.nn.log_sigmoid(_dot(h, wf_ref[...]) + bf_ref[...])
    logf_ref[...] = logf

    if folded_bias:
        (carry_ref,) = scratch

        @pl.when(pl.program_id(1) == 0)
        def _():
            carry_ref[...] = jnp.zeros_like(carry_ref)

        c = _dot_ones_lhs(_tri(tm, upper=False), logf) + carry_ref[...]
        carry_ref[...] = c[tm - 1:tm, :]
        c = c * LOG2E
        lane = lax.broadcasted_iota(jnp.int32, (tm, dh), 1)
        third = lane - jnp.where(lane >= 3, 3, 0)
        in_q, in_k = lane < 3, (lane >= 3) & (lane < 6)
        ones_q = jnp.where(in_k, 1.0, 0.0)
        ones_k = jnp.where(in_q, 1.0, 0.0)
        for hd, _ in heads:
            hi, mid, lo = _bias_columns(c, hd, dh)
            terms = jnp.where(third == 0, hi, jnp.where(third == 1, mid, lo))
            q_ref[0, hd, dh:2 * dh, :] = jnp.where(in_q, terms, ones_q).T.astype(BF16)
            k_ref[0, hd, :, dh:2 * dh] = jnp.where(in_k, -terms, ones_k).astype(BF16)

    q = _dot(h, w_ref[:, 0:e]) * scale
    for hd, hs in heads:
        if folded_bias:
            q_ref[0, hd, 0:dh, :] = q[:, hs].T.astype(BF16)
        else:
            q_ref[0, hd] = q[:, hs].astype(BF16)
    k = _dot(h, w_ref[:, e:2 * e])
    for hd, hs in heads:
        kf_ref.at[0, 0][pl.ds(hd, tm, stride=nh), :] = k[:, hs]
        if folded_bias:
            k_ref[0, hd, :, 0:dh] = k[:, hs].astype(BF16)
        else:
            k_ref[0, hd] = k[:, hs].astype(BF16)
    v = _dot(h, w_ref[:, 2 * e:3 * e])
    for hd, hs in heads:
        vf_ref.at[0, 0][pl.ds(hd, tm, stride=nh), :] = v[:, hs]
        v_ref[0, hd] = (v[:, hs].T if folded_bias else v[:, hs]).astype(BF16)
    sz_ref[...] = _silu(_dot(h, w_ref[:, 3 * e:4 * e])).astype(BF16)


def _fox_proj(x, gpre, w, wf, bf, kv_leaves, *, layer, groups, rows, heads, tokens, folded_bias):
    n, d = x.shape
    e = (w.shape[1]) // 4
    dh = e // heads
    assert n == groups * rows and rows % tokens == 0
    steps = rows // tokens
    tok = lambda b, j: (b * steps + j, 0)
    if folded_bias:
        q_shape = jax.ShapeDtypeStruct((groups, heads, 2 * dh, rows), BF16)
        q_spec = pl.BlockSpec((1, heads, 2 * dh, tokens), lambda b, j: (b, 0, 0, j))
        k_shape = jax.ShapeDtypeStruct((groups, heads, rows, 2 * dh), BF16)
        k_spec = pl.BlockSpec((1, heads, tokens, 2 * dh), lambda b, j: (b, 0, j, 0))
        v_shape = jax.ShapeDtypeStruct((groups, heads, dh, rows), BF16)
        v_spec = pl.BlockSpec((1, heads, dh, tokens), lambda b, j: (b, 0, 0, j))
        scratch = [pltpu.VMEM((1, heads), F32)]
        scale = float(dh) ** -0.5 * LOG2E
    else:
        q_shape = k_shape = v_shape = jax.ShapeDtypeStruct((groups, heads, rows, dh), BF16)
        q_spec = k_spec = v_spec = pl.BlockSpec((1, heads, tokens, dh), lambda b, j: (b, 0, j, 0))
        scratch = []
        scale = float(dh) ** -0.5
    k_leaf, v_leaf = kv_leaves
    assert k_leaf.shape == v_leaf.shape == (k_leaf.shape[0], groups, rows * heads, dh)
    leaf_shape = jax.ShapeDtypeStruct(k_leaf.shape, F32)
    leaf_spec = pl.BlockSpec((1, 1, tokens * heads, dh), lambda b, j: (layer, b, j, 0))
    return pl.pallas_call(
        functools.partial(_fox_proj_kernel, scale=scale, folded_bias=folded_bias),
        grid=(groups, steps),
        in_specs=[pl.BlockSpec((tokens, d), tok), _resident(gpre.shape), _resident(w.shape),
                  _resident(wf.shape), _resident(bf.shape),
                  pl.BlockSpec(memory_space=pl.ANY), pl.BlockSpec(memory_space=pl.ANY)],
        out_specs=[q_spec, k_spec, v_spec, leaf_spec, leaf_spec,
                   pl.BlockSpec((tokens, e), tok), pl.BlockSpec((tokens, heads), tok)],
        out_shape=[q_shape, k_shape, v_shape, leaf_shape, leaf_shape,
                   jax.ShapeDtypeStruct((n, e), BF16), jax.ShapeDtypeStruct((n, heads), F32)],
        input_output_aliases={5: 3, 6: 4},
        scratch_shapes=scratch,
        compiler_params=_params(2),
        name="fox_proj",
    )(x, gpre, w, wf, bf, k_leaf, v_leaf)


def _fox_attn_kernel(qt_ref, qt_next_ref, k_ref, vt_ref, o_ref, s_ref, smax_ref, m_ref, l_ref, acc_ref):
    hpb, tq = qt_ref.shape[1], qt_ref.shape[3]
    tk = s_ref.shape[2]
    qi = pl.program_id(2)

    def produce(j, slot, hh, queries_ref=qt_ref):
        k0 = pl.multiple_of(j * tk, tk)
        s = _dot(k_ref[0, hh, pl.ds(k0, tk), :], queries_ref[0, hh])
        s_ref[slot, hh] = s
        smax_ref[slot, hh] = jnp.max(s, axis=0, keepdims=True)

    def consume_full(j, slot, hh):
        k0 = pl.multiple_of(j * tk, tk)
        m = m_ref[hh]
        m_new = jnp.maximum(m, smax_ref[slot, hh])
        alpha = jnp.exp2(m - m_new)
        p = jnp.exp2(s_ref[slot, hh] - m_new)
        m_ref[hh] = m_new
        l_ref[hh] = alpha * l_ref[hh] + jnp.sum(p, axis=0, keepdims=True)
        acc_ref[hh] = alpha * acc_ref[hh] + _dot(vt_ref[0, hh, :, pl.ds(k0, tk)], p.astype(BF16))

    def consume_diagonal(j, slot, hh):
        k0 = pl.multiple_of(j * tk, tk)
        sub = ATTN_CAUSAL_BLOCK
        on_or_below = (lax.broadcasted_iota(jnp.int32, (sub, sub), 0)
                       <= lax.broadcasted_iota(jnp.int32, (sub, sub), 1))
        for b in range(tq // sub):
            cols = slice(b * sub, (b + 1) * sub)
            edge = jnp.where(on_or_below, s_ref[slot, hh, b * sub:(b + 1) * sub, cols], -jnp.inf)
            s_max = jnp.max(edge, axis=0, keepdims=True)
            if b:
                whole = s_ref[slot, hh, 0:b * sub, cols]
                s_max = jnp.maximum(s_max, jnp.max(whole, axis=0, keepdims=True))
            m = m_ref[hh, :, cols]
            m_new = jnp.maximum(m, s_max)
            alpha = jnp.exp2(m - m_new)
            p = jnp.exp2(edge - m_new)
            l_new = jnp.sum(p, axis=0, keepdims=True)
            pv = _dot(vt_ref[0, hh, :, pl.ds(pl.multiple_of(k0 + b * sub, sub), sub)], p.astype(BF16))
            if b:
                p = jnp.exp2(whole - m_new)
                l_new = l_new + jnp.sum(p, axis=0, keepdims=True)
                pv = pv + _dot(vt_ref[0, hh, :, pl.ds(k0, b * sub)], p.astype(BF16))
            m_ref[hh, :, cols] = m_new
            l_ref[hh, :, cols] = alpha * l_ref[hh, :, cols] + l_new
            acc_ref[hh, :, cols] = alpha * acc_ref[hh, :, cols] + pv

    def consume(j, slot, hh, diagonal):
        (consume_diagonal if diagonal else consume_full)(j, slot, hh)

    def step(produced, consumed):
        for hh in range(hpb):
            if produced is not None:
                produce(produced[0], produced[1], hh, *produced[2:])
            if consumed is not None:
                consume(*consumed[:2], hh, consumed[2])

    m_ref[...] = jnp.full_like(m_ref, -jnp.inf)
    l_ref[...] = jnp.zeros_like(l_ref)
    acc_ref[...] = jnp.zeros_like(acc_ref)
    n_full = qi
    first = s_ref.shape[0] - 1
    hand_over = (0, first, qt_next_ref)

    @pl.when(qi == 0)
    def _():
        step((0, 0), None)
        step(hand_over, (0, 0, True))

    @pl.when(qi > 0)
    def _():
        step((1, 1), (0, first, False))

        @pl.loop(0, (n_full - 1) // 2)
        def _(i):
            step((2 * i + 2, 0), (2 * i + 1, 1, False))
            step((2 * i + 3, 1), (2 * i + 2, 0, False))

        @pl.when(n_full % 2 == 1)
        def _():
            step(hand_over, (n_full, 1, True))

        @pl.when(n_full % 2 == 0)
        def _():
            step((n_full, 0), (n_full - 1, 1, False))
            step(hand_over, (n_full, 0, True))

    dh = acc_ref.shape[1]
    for hh in range(hpb):
        o_ref[0, :, hh * dh:(hh + 1) * dh] = (acc_ref[hh] / l_ref[hh]).T.astype(BF16)


def _fox_attn(qt, k, vt):
    b, nh, s, dh2 = k.shape
    dh = vt.shape[2]
    t, tk, hpb = ATTN_QUERIES, ATTN_KEYS, ATTN_HEADS
    assert s % tk == 0 and tk == t and t % ATTN_CAUSAL_BLOCK == 0
    assert nh % hpb == 0 and dh == LANES and dh2 == 2 * dh
    last = s // t - 1
    return pl.pallas_call(
        _fox_attn_kernel,
        grid=(b, nh // hpb, s // t),
        in_specs=[
            pl.BlockSpec((1, hpb, dh2, t), lambda bi, hi, qi: (bi, hi, 0, qi)),
            pl.BlockSpec((1, hpb, dh2, t), lambda bi, hi, qi: (bi, hi, 0, jnp.minimum(qi + 1, last))),
            pl.BlockSpec((1, hpb, s, dh2), lambda bi, hi, qi: (bi, hi, 0, 0)),
            pl.BlockSpec((1, hpb, dh, s), lambda bi, hi, qi: (bi, hi, 0, 0)),
        ],
        out_specs=pl.BlockSpec((1, t, hpb * dh), lambda bi, hi, qi: (bi, qi, hi)),
        out_shape=jax.ShapeDtypeStruct((b, s, nh * dh), BF16),
        scratch_shapes=[pltpu.VMEM((3, hpb, tk, t), F32), pltpu.VMEM((3, hpb, 1, t), F32),
                        pltpu.VMEM((hpb, 1, t), F32), pltpu.VMEM((hpb, 1, t), F32),
                        pltpu.VMEM((hpb, dh, t), F32)],
        compiler_params=_params(3),
        name="fox_attn",
    )(qt, qt, k, vt)


def _fox_sample_attn_kernel(q_ref, kc_ref, vc_ref, kn_ref, vn_ref, lfct_ref, lfn_ref, lfnt_ref,
                            o_ref, cct_ref, cnt_ref, cn_ref, m_ref, l_ref, acc_ref, s_ref, sn_ref, pv_ref):
    nh, t, dh = q_ref.shape[1], q_ref.shape[2], q_ref.shape[3]
    past = lfct_ref.shape[2]
    pad = lfnt_ref.shape[2]
    pc = kc_ref.shape[2] // nh
    step = pl.program_id(1)

    @pl.when(step == 0)
    def _():
        triu = _tri(CUMSUM_CHUNK, upper=True)
        total = jnp.zeros((nh, 1), F32)
        for i in range(past // CUMSUM_CHUNK):
            cs = slice(i * CUMSUM_CHUNK, (i + 1) * CUMSUM_CHUNK)
            blk = _dot_ones_rhs(lfct_ref[0, :, cs], triu) + total
            cct_ref[:, cs] = blk
            total = blk[:, CUMSUM_CHUNK - 1:CUMSUM_CHUNK]
        eye = (lax.broadcasted_iota(jnp.int32, (nh, nh), 0)
               == lax.broadcasted_iota(jnp.int32, (nh, nh), 1))
        total_row = jnp.sum(jnp.where(eye, total, 0.0), axis=0, keepdims=True)
        cn_ref[...] = _dot_ones_lhs(_tri(t, upper=False), lfn_ref[0]) + total_row
        cnt_ref[...] = _dot_ones_rhs(lfnt_ref[0], _tri(pad, upper=True)) + total
        m_ref[...] = jnp.full_like(m_ref, -jnp.inf)
        l_ref[...] = jnp.zeros_like(l_ref)
        acc_ref[...] = jnp.zeros_like(acc_ref)

    def rows(hd):
        return slice(hd * t, (hd + 1) * t)

    def attend(s, values):
        m = m_ref[...]
        m_new = jnp.maximum(m, jnp.max(s, axis=1, keepdims=True))
        alpha = jnp.exp(m - m_new)
        p = jnp.exp(s - m_new)
        m_ref[...] = m_new
        l_ref[...] = alpha * l_ref[...] + jnp.sum(p, axis=1, keepdims=True)
        p = p.astype(BF16)
        for hd in range(nh):
            pv_ref[rows(hd), :] = _dot(p[rows(hd), :], values(hd))
        acc_ref[...] = alpha * acc_ref[...] + pv_ref[...]

    p0 = pl.multiple_of(step * pc, pc)
    for hd in range(nh):
        kc = kc_ref.at[0, 0][pl.ds(hd, pc, stride=nh), :].astype(BF16)
        s_ref[rows(hd), :] = (lax.dot_general(q_ref[0, hd], kc, _NT, preferred_element_type=F32)
                              + cn_ref[:, hd:hd + 1] - cct_ref[hd:hd + 1, pl.ds(p0, pc)])
    attend(s_ref[...], lambda hd: vc_ref.at[0, 0][pl.ds(hd, pc, stride=nh), :].astype(BF16))

    @pl.when(step == pl.num_programs(1) - 1)
    def _():
        visible = (lax.broadcasted_iota(jnp.int32, (t, pad), 1)
                   <= lax.broadcasted_iota(jnp.int32, (t, pad), 0))
        zeros = jnp.zeros((pad - t, dh), BF16)
        for hd in range(nh):
            kn = jnp.concatenate([kn_ref[0, hd], zeros], axis=0)
            s = (lax.dot_general(q_ref[0, hd], kn, _NT, preferred_element_type=F32)
                 + cn_ref[:, hd:hd + 1] - cnt_ref[hd:hd + 1, :])
            sn_ref[rows(hd), :] = jnp.where(visible, s, -jnp.inf)
        attend(sn_ref[...], lambda hd: jnp.concatenate([vn_ref[0, hd], zeros], axis=0))
        o = (acc_ref[...] / l_ref[...]).astype(BF16)
        for hd in range(nh):
            o_ref[:, hd * dh:(hd + 1) * dh] = o[rows(hd), :]


def _fox_sample_attn(q, kn, vn, k_cache, v_cache, layer, logf_cache_t, logf_new, logf_new_t, *, batch):
    _, nh, n, dh = q.shape
    t = n // batch
    past = k_cache.shape[2] // nh
    pc = CACHE_CHUNK
    assert past % pc == 0 and past % CUMSUM_CHUNK == 0 and dh == LANES
    new_spec = pl.BlockSpec((1, nh, t, dh), lambda bi, si: (0, 0, bi, 0))
    cache_spec = pl.BlockSpec((1, 1, pc * nh, dh), lambda bi, si: (layer, bi, si, 0))
    whole = lambda shape: pl.BlockSpec((1,) + shape[1:], lambda bi, si: (bi, 0, 0))
    pad = logf_new_t.shape[2]
    return pl.pallas_call(
        _fox_sample_attn_kernel,
        grid=(batch, past // pc),
        in_specs=[new_spec, cache_spec, cache_spec, new_spec, new_spec,
                  whole(logf_cache_t.shape), whole(logf_new.shape), whole(logf_new_t.shape)],
        out_specs=pl.BlockSpec((t, nh * dh), lambda bi, si: (bi, 0)),
        out_shape=jax.ShapeDtypeStruct((n, nh * dh), BF16),
        scratch_shapes=[pltpu.VMEM((nh, past), F32), pltpu.VMEM((nh, pad), F32),
                        pltpu.VMEM((t, nh), F32), pltpu.VMEM((nh * t, 1), F32),
                        pltpu.VMEM((nh * t, 1), F32), pltpu.VMEM((nh * t, dh), F32),
                        pltpu.VMEM((nh * t, pc), F32), pltpu.VMEM((nh * t, pad), F32),
                        pltpu.VMEM((nh * t, dh), F32)],
        compiler_params=_params(2),
        name="fox_sample_attn",
    )(q, k_cache, v_cache, kn, vn, logf_cache_t, logf_new, logf_new_t)


def _fox_post_kernel(o_ref, sz_ref, x_ref, p_ref, gpost_ref, wout_ref, wgate_ref, wproj_ref, out_ref):
    o = _dot(o_ref[...] * sz_ref[...], wout_ref[...])
    out_ref[...] = _residual_and_embedding(x_ref[...], o, p_ref[...], gpost_ref, wgate_ref, wproj_ref)


def _fox_post(o, sz, x, p, gpost, w_out, w_gate, w_proj, *, layer, tokens):
    n, d = x.shape
    e = o.shape[1]
    assert n % tokens == 0
    row = lambda i: (i, 0)
    return pl.pallas_call(
        _fox_post_kernel,
        grid=(n // tokens,),
        in_specs=[pl.BlockSpec((tokens, e), row), pl.BlockSpec((tokens, e), row),
                  pl.BlockSpec((tokens, d), row),
                  pl.BlockSpec((None, tokens, p.shape[2]), lambda i: (layer, i, 0)),
                  _resident(gpost.shape), _resident(w_out.shape), _resident(w_gate.shape),
                  _resident(w_proj.shape)],
        out_specs=pl.BlockSpec((tokens, d), row),
        out_shape=jax.ShapeDtypeStruct((n, d), F32),
        compiler_params=_params(1),
        name="fox_post",
    )(o, sz, x, p, gpost, w_out, w_gate, w_proj)


def _mix_operands(w_s, b_s, rows, gd):
    n = min(rows, GMLP_BLOCK)
    reps = GMLP_BLOCK // n
    w = jnp.tile(w_s[:, :n, :n], (1, reps, reps))
    b = jnp.repeat(jnp.tile(b_s[:, :n], (1, reps)).T, gd, axis=1)
    return w, b, n


def kernel(x_prompt, x_sample, cache_fox_k, cache_fox_v, cache_fox_logf, p_prompt, p_sample, norm_pre, norm_post, gmlp_w_in, gmlp_ln_g, gmlp_ln_b, gmlp_w_s, gmlp_b_s, gmlp_w_out, fox_w_in, fox_b_f, fox_w_out, ple_w_proj, ple_w_gate):
    bp, sp, d = x_prompt.shape
    bs, ts, _ = x_sample.shape
    depth = norm_pre.shape[0]
    n_fox, _, past, nh, dh = cache_fox_k.shape
    e = nh * dh
    groups = gmlp_w_s.shape[1]
    gd = e // groups
    n_p, n_s = bp * sp, bs * ts

    xp = x_prompt.reshape(n_p, d)
    xs = x_sample.reshape(n_s, d)
    cache_k = cache_fox_k.reshape(cache_fox_k.shape[0], bs, past * nh, dh)
    cache_v = cache_fox_v.reshape(cache_fox_v.shape[0], bs, past * nh, dh)
    pp = p_prompt.reshape(depth, n_p, -1)
    ps = p_sample.reshape(depth, n_s, -1)
    gmlp_v_s = []
    flf_p, flf_s = [], []
    kv_p = None
    kv_s = (jnp.zeros((n_fox, 1, n_s * nh, dh), F32),) * 2
    for i in range(depth):
        j = i // 2
        gpre, gpost = norm_pre[i][None], norm_post[i][None]
        w_gate, w_proj = ple_w_gate[i].astype(BF16), ple_w_proj[i].astype(BF16)
        if i % 2 == 0:
            w_in, w_out = gmlp_w_in[j].astype(BF16), gmlp_w_out[j].astype(BF16)
            ln_g, ln_b = gmlp_ln_g[j][None], gmlp_ln_b[j][None]
            wm_p, bm_p, _ = _mix_operands(gmlp_w_s[j], gmlp_b_s[j], sp, gd)
            wm_s, bm_s, n_mix = _mix_operands(gmlp_w_s[j], gmlp_b_s[j], ts, gd)
            leaves = [(n_fox * n_p * nh, dh)] * 2 if kv_p is None else []
            xp, _, *zeroed = _gmlp_layer(xp, pp, gpre, gpost, w_in, ln_g, ln_b, wm_p, bm_p, w_out, w_gate,
                                         w_proj, layer=i, tokens=GMLP_TOKENS, span=CHUNK, causal=True,
                                         emit_v=False, zero_buffers=leaves)
            if zeroed:
                kv_p = [z.reshape(n_fox, bp, sp * nh, dh) for z in zeroed]
            assert n_mix <= CHUNK and n_s % GMLP_BLOCK == 0
            xs, v_s = _gmlp_layer(xs, ps, gpre, gpost, w_in, ln_g, ln_b, wm_s, bm_s, w_out, w_gate, w_proj,
                                  layer=i, tokens=GMLP_BLOCK, span=n_mix, causal=False, emit_v=True)
            gmlp_v_s.append(v_s.reshape(bs, ts, e))
        else:
            w = fox_w_in[j][:, :4 * e].astype(BF16)
            wf = fox_w_in[j][:, 4 * e:].astype(BF16)
            bf = fox_b_f[j][None]
            w_out = fox_w_out[j].astype(BF16)

            q, kb, vt, *kv_p, sz, lf = _fox_proj(xp, gpre, w, wf, bf, kv_p, layer=j, groups=bp, rows=sp,
                                                 heads=nh, tokens=PROJ_TOKENS, folded_bias=True)
            o = _fox_attn(q, kb, vt)
            xp = _fox_post(o.reshape(n_p, e), sz, xp, pp, gpost, w_out, w_gate, w_proj, layer=i,
                           tokens=POST_TOKENS)
            flf_p.append(lf.reshape(bp, sp, nh))

            q, kb, vb, *kv_s, sz, lf = _fox_proj(xs, gpre, w, wf, bf, kv_s, layer=j, groups=1, rows=n_s,
                                                 heads=nh, tokens=n_s, folded_bias=False)
            lf3 = lf.reshape(bs, ts, nh)
            lf3_t = jnp.pad(lf3.transpose(0, 2, 1), ((0, 0), (0, 0), (0, LANES - ts)))
            o = _fox_sample_attn(q, kb, vb, cache_k, cache_v, j, cache_fox_logf[j].transpose(0, 2, 1),
                                 lf3, lf3_t, batch=bs)
            xs = _fox_post(o, sz, xs, ps, gpost, w_out, w_gate, w_proj, layer=i, tokens=n_s)
            flf_s.append(lf3)
    fk_p, fv_p = (a.reshape(n_fox, bp, sp, nh, dh) for a in kv_p)
    fk_s, fv_s = (a.reshape(n_fox, bs, ts, nh, dh) for a in kv_s)
    return (xp.reshape(bp, sp, d), xs.reshape(bs, ts, d), jnp.stack(gmlp_v_s),
            fk_p, fv_p, jnp.stack(flf_p), fk_s, fv_s, jnp.stack(flf_s))
```
